```python
import math
import jax, jax.numpy as jnp
from jax import lax
import numpy as np

D_MODEL = 2048
BATCH = 1
SEQ = 16384
DEPTH = 2

GRID_W = 64
CTX_LEN = 256
Q_BLOCK = 128
ROPE_BASE = 10000.0
LN_EPS = 1e-5
RMS_EPS = 1e-6

DA_HEADS = 4
DA_HEAD_DIM = 64
DA_WIDTH = DA_HEADS * 2 * DA_HEAD_DIM

LRU_WIDTH = 512
LRU_BLOCKS = 4
LRU_BLOCK = LRU_WIDTH // LRU_BLOCKS
CONV_W = 4
CONV_LEFT = 2
LRU_C = 8.0

MLA_HEADS = 4
MLA_Q_RANK = 384
MLA_KV_RANK = 256
MLA_NOPE = 128
MLA_ROPE = 64
MLA_V = 128
MLA_WIDTH = MLA_HEADS * MLA_V

FFT_GROUPS = 4
FFT_GROUP = 128
FFT_WIDTH = FFT_GROUPS * FFT_GROUP

MIX_WIDTH = DA_WIDTH + LRU_WIDTH + MLA_WIDTH + FFT_WIDTH

IN_DA = 3 * DA_WIDTH
IN_LRU = 2 * LRU_WIDTH
IN_MLA = MLA_Q_RANK + MLA_KV_RANK + MLA_ROPE
IN_FFT = FFT_WIDTH
OFF_LRU = IN_DA
OFF_MLA = OFF_LRU + IN_LRU
OFF_FFT = OFF_MLA + IN_MLA
IN_WIDTH = OFF_FFT + IN_FFT

D_FF = ((8 * D_MODEL + 3 * 256 - 1) // (3 * 256)) * 256

DEEPNORM_ALPHA = (2 * DEPTH) ** 0.25
DEEPNORM_BETA = (8 * DEPTH) ** -0.25

kernel_name = "hymba_style_hybrid_dit_block"


def _layernorm(x, g, b):
    xf = x.astype(jnp.float32)
    mu = jnp.mean(xf, axis=-1, keepdims=True)
    var = jnp.mean(jnp.square(xf - mu), axis=-1, keepdims=True)
    return ((xf - mu) * lax.rsqrt(var + LN_EPS) * g + b).astype(x.dtype)


def _rmsnorm(x, g):
    xf = x.astype(jnp.float32)
    inv = lax.rsqrt(jnp.mean(jnp.square(xf), axis=-1, keepdims=True) + RMS_EPS)
    return (xf * inv * g).astype(x.dtype)


def _modulate(x, shift, scale):
    return x * (1.0 + scale) + shift


def _ada_mods(cvec, w, b):
    m = jax.nn.silu(cvec) @ w + b
    return jnp.split(m, 6, axis=-1)


def _axial_rope_tables(n, rot_dim):
    rows = n // GRID_W
    row = jnp.repeat(jnp.arange(rows, dtype=jnp.float32), GRID_W)
    col = jnp.tile(jnp.arange(GRID_W, dtype=jnp.float32), rows)
    axis_dim = rot_dim // 2
    inv = ROPE_BASE ** (-jnp.arange(0, axis_dim, 2, dtype=jnp.float32) / axis_dim)
    ang = jnp.concatenate([row[:, None] * inv, col[:, None] * inv], axis=-1)
    return jnp.cos(ang), jnp.sin(ang)


def _rope(x, cos, sin):
    half = x.shape[-1] // 2
    bshape = (1, x.shape[1]) + (1,) * (x.ndim - 3) + (half,)
    c = cos.reshape(bshape)
    s = sin.reshape(bshape)
    x1 = x[..., :half].astype(jnp.float32)
    x2 = x[..., half:].astype(jnp.float32)
    return jnp.concatenate([x1 * c - x2 * s, x1 * s + x2 * c], axis=-1).astype(x.dtype)


def _sweep_query_blocks(fn, qs):
    b, n = qs[0].shape[:2]
    nb = n // Q_BLOCK
    blk = tuple(jnp.moveaxis(q.reshape((b, nb, Q_BLOCK) + q.shape[2:]), 1, 0) for q in qs)
    out = lax.map(lambda a: fn(*a), blk)
    out = jnp.moveaxis(out, 0, 1)
    return out.reshape((b, n) + out.shape[3:])


def _probs(q, k, scale):
    s = jnp.einsum("bqhd,bkhd->bhqk", q, k, preferred_element_type=jnp.float32) * scale
    return jax.nn.softmax(s, axis=-1)


def _diff_attention(u_lat, u_ctx, cos, sin, lq1, lk1, lq2, lk2, subln_g, lambda_init, need_ctx):
    def split(u):
        b, n, _ = u.shape
        q = u[..., :DA_WIDTH].reshape(b, n, DA_HEADS, 2, DA_HEAD_DIM)
        k = u[..., DA_WIDTH:2 * DA_WIDTH].reshape(b, n, DA_HEADS, 2, DA_HEAD_DIM)
        v = u[..., 2 * DA_WIDTH:].reshape(b, n, DA_HEADS, 2 * DA_HEAD_DIM)
        return q, k, v

    q_l, k_l, v_l = split(u_lat)
    q_c, k_c, v_c = split(u_ctx)
    q_l = _rope(q_l, cos, sin)
    k_l = _rope(k_l, cos, sin)
    f32 = jnp.float32
    lam = (jnp.exp(jnp.sum(lq1.astype(f32) * lk1.astype(f32)))
           - jnp.exp(jnp.sum(lq2.astype(f32) * lk2.astype(f32))) + lambda_init)
    scale = DA_HEAD_DIM ** -0.5
    k_all = jnp.concatenate([k_c, k_l], axis=1)
    v_all = jnp.concatenate([v_c, v_l], axis=1)

    def core(q, k, v):
        p = _probs(q[..., 0, :], k[..., 0, :], scale) - lam * _probs(q[..., 1, :], k[..., 1, :], scale)
        return jnp.einsum("bhqk,bkhd->bqhd", p, v.astype(f32))

    def post(o):
        b, n = o.shape[:2]
        return (_rmsnorm(o, subln_g) * (1.0 - lambda_init)).reshape(b, n, DA_WIDTH)

    y_lat = post(_sweep_query_blocks(lambda qb: core(qb, k_all, v_all), (q_l,)))
    y_ctx = post(core(q_c, k_c, v_c)) if need_ctx else None
    return y_lat, y_ctx


def _depthwise_conv(x, w, b):
    rhs = w.reshape(CONV_W, 1, x.shape[-1]).astype(x.dtype)
    y = lax.conv_general_dilated(x, rhs, window_strides=(1,),
                                 padding=[(CONV_LEFT, CONV_W - 1 - CONV_LEFT)],
                                 dimension_numbers=("NWC", "WIO", "NWC"),
                                 feature_group_count=x.shape[-1])
    return y + b


def _linear_combine(e1, e2):
    return e1[0] * e2[0], e2[0] * e1[1] + e2[1]


def _rglru_scan(x, h0, w_r, b_r, w_i, b_i, lam, reverse):
    b, n, _ = x.shape
    xb = x.reshape(b, n, LRU_BLOCKS, LRU_BLOCK)

    def gate(w, bias):
        z = jnp.einsum("bnhi,hij->bnhj", xb, w, preferred_element_type=jnp.float32)
        return jax.nn.sigmoid(z.reshape(b, n, LRU_WIDTH) + bias)

    r = gate(w_r, b_r)
    i = gate(w_i, b_i)
    log_a = -LRU_C * r * jax.nn.softplus(-lam.astype(jnp.float32))
    a = jnp.exp(log_a)
    u = jnp.sqrt(-jnp.expm1(2.0 * log_a)) * (i * x.astype(jnp.float32))
    if reverse:
        a, u = jnp.flip(a, 1), jnp.flip(u, 1)
    u = u.at[:, 0].add(a[:, 0] * h0)
    _, h = lax.associative_scan(_linear_combine, (a, u), axis=1)
    h_last = h[:, -1]
    if reverse:
        h = jnp.flip(h, 1)
    return h, h_last


def _rglru_mixer(u_lat, u_ctx, conv_w, conv_b, wr, br, wi, bi, lam, need_ctx):
    x_l = _depthwise_conv(u_lat[..., :LRU_WIDTH], conv_w, conv_b)
    x_c = _depthwise_conv(u_ctx[..., :LRU_WIDTH], conv_w, conv_b)
    h0 = jnp.zeros((u_ctx.shape[0], LRU_WIDTH), jnp.float32)
    hc_f, s_f = _rglru_scan(x_c, h0, wr[0], br[0], wi[0], bi[0], lam[0], reverse=False)
    hc_b, s_b = _rglru_scan(x_c, h0, wr[1], br[1], wi[1], bi[1], lam[1], reverse=True)
    hl_f, _ = _rglru_scan(x_l, s_f, wr[0], br[0], wi[0], bi[0], lam[0], reverse=False)
    hl_b, _ = _rglru_scan(x_l, s_b, wr[1], br[1], wi[1], bi[1], lam[1], reverse=True)
    y_lat = (hl_f + hl_b) * jax.nn.gelu(u_lat[..., LRU_WIDTH:].astype(jnp.float32))
    y_ctx = ((hc_f + hc_b) * jax.nn.gelu(u_ctx[..., LRU_WIDTH:].astype(jnp.float32))) if need_ctx else None
    return y_lat, y_ctx


def _mla(u_lat, u_ctx, cos, sin, qn_g, w_uq, kvn_g, w_ukv, need_ctx):
    def qkv(u, rotate):
        b, n, _ = u.shape
        c_q = u[..., :MLA_Q_RANK]
        c_kv = u[..., MLA_Q_RANK:MLA_Q_RANK + MLA_KV_RANK]
        k_r = u[..., MLA_Q_RANK + MLA_KV_RANK:][:, :, None, :]
        q = (_rmsnorm(c_q, qn_g) @ w_uq).reshape(b, n, MLA_HEADS, MLA_NOPE + MLA_ROPE)
        kv = (_rmsnorm(c_kv, kvn_g) @ w_ukv).reshape(b, n, MLA_HEADS, MLA_NOPE + MLA_V)
        q_nope, q_rope = q[..., :MLA_NOPE], q[..., MLA_NOPE:]
        k_nope, v = kv[..., :MLA_NOPE], kv[..., MLA_NOPE:]
        if rotate:
            q_rope = _rope(q_rope, cos, sin)
            k_r = _rope(k_r, cos, sin)
        q = jnp.concatenate([q_nope, q_rope], axis=-1)
        k = jnp.concatenate([k_nope, jnp.broadcast_to(k_r, (b, n, MLA_HEADS, MLA_ROPE))], axis=-1)
        return q, k, v

    q_l, k_l, v_l = qkv(u_lat, True)
    q_c, k_c, v_c = qkv(u_ctx, False)
    scale = (MLA_NOPE + MLA_ROPE) ** -0.5
    k_all = jnp.concatenate([k_c, k_l], axis=1)
    v_all = jnp.concatenate([v_c, v_l], axis=1)

    def core(q, k, v):
        return jnp.einsum("bhqk,bkhd->bqhd", _probs(q, k, scale), v.astype(jnp.float32))

    b, n = u_lat.shape[:2]
    y_lat = _sweep_query_blocks(lambda qb: core(qb, k_all, v_all), (q_l,)).reshape(b, n, MLA_WIDTH)
    y_ctx = core(q_c, k_c, v_c).reshape(b, u_ctx.shape[1], MLA_WIDTH) if need_ctx else None
    return y_lat, y_ctx


def _fourier(u):
    b, n, _ = u.shape
    g = u.astype(jnp.float32).reshape(b, n, FFT_GROUPS, FFT_GROUP)
    return jnp.real(jnp.fft.fft2(g, axes=(1, 3), norm="ortho")).reshape(b, n, FFT_WIDTH)


def _swiglu(h, wg, wu, wd):
    return (jax.nn.silu(h @ wg) * (h @ wu)) @ wd


def setup_inputs(seed: int = 0) -> dict:
    key = jax.random.key(seed)
    ks = iter(jax.random.split(key, 32))

    def nrm(shape, s):
        return jax.random.normal(next(ks), shape, jnp.float32) * s

    L, D = DEPTH, D_MODEL
    inp = {}
    inp["x"] = nrm((BATCH, SEQ, D), 1.0)
    inp["c"] = nrm((BATCH, D), 1.0)
    inp["ctx"] = nrm((BATCH, CTX_LEN, D), 1.0)
    inp["c_ctx"] = nrm((D,), 1.0)
    inp["w_ada"] = nrm((L, D, 6 * D), 0.5 * D ** -0.5)
    inp["b_ada"] = nrm((L, 6 * D), 0.02)
    inp["w_in"] = nrm((L, D, IN_WIDTH), D ** -0.5)
    inp["w_out"] = nrm((L, MIX_WIDTH, D), DEEPNORM_BETA * MIX_WIDTH ** -0.5)
    inp["ln1_g"] = 1.0 + nrm((L, D), 0.02)
    inp["ln1_b"] = nrm((L, D), 0.02)
    inp["ln2_g"] = 1.0 + nrm((L, D), 0.02)
    inp["ln2_b"] = nrm((L, D), 0.02)
    inp["da_lq1"] = nrm((L, DA_HEAD_DIM), 0.1)
    inp["da_lk1"] = nrm((L, DA_HEAD_DIM), 0.1)
    inp["da_lq2"] = nrm((L, DA_HEAD_DIM), 0.1)
    inp["da_lk2"] = nrm((L, DA_HEAD_DIM), 0.1)
    inp["da_subln_g"] = 1.0 + nrm((L, 2 * DA_HEAD_DIM), 0.02)
    inp["lru_conv_w"] = nrm((L, CONV_W, LRU_WIDTH), CONV_W ** -0.5)
    inp["lru_conv_b"] = nrm((L, LRU_WIDTH), 0.02)
    inp["lru_wr"] = nrm((L, 2, LRU_BLOCKS, LRU_BLOCK, LRU_BLOCK), LRU_BLOCK ** -0.5)
    inp["lru_br"] = nrm((L, 2, LRU_WIDTH), 0.02)
    inp["lru_wi"] = nrm((L, 2, LRU_BLOCKS, LRU_BLOCK, LRU_BLOCK), LRU_BLOCK ** -0.5)
    inp["lru_bi"] = nrm((L, 2, LRU_WIDTH), 0.02)
    a_c = jax.random.uniform(next(ks), (L, 2, LRU_WIDTH), jnp.float32, 0.9, 0.999)
    a = a_c ** (1.0 / LRU_C)
    inp["lru_lam"] = jnp.log(a) - jnp.log1p(-a)
    inp["mla_qn_g"] = 1.0 + nrm((L, MLA_Q_RANK), 0.02)
    inp["mla_wuq"] = nrm((L, MLA_Q_RANK, MLA_HEADS * (MLA_NOPE + MLA_ROPE)), MLA_Q_RANK ** -0.5)
    inp["mla_kvn_g"] = 1.0 + nrm((L, MLA_KV_RANK), 0.02)
    inp["mla_wukv"] = nrm((L, MLA_KV_RANK, MLA_HEADS * (MLA_NOPE + MLA_V)), MLA_KV_RANK ** -0.5)
    inp["ffn_wg"] = nrm((L, D, D_FF), D ** -0.5)
    inp["ffn_wu"] = nrm((L, D, D_FF), D ** -0.5)
    inp["ffn_wd"] = nrm((L, D_FF, D), DEEPNORM_BETA * D_FF ** -0.5)
    return inp


def reference(x, c, ctx, c_ctx, w_ada, b_ada, w_in, w_out, ln1_g, ln1_b, ln2_g, ln2_b,
              da_lq1, da_lk1, da_lq2, da_lk2, da_subln_g,
              lru_conv_w, lru_conv_b, lru_wr, lru_br, lru_wi, lru_bi, lru_lam,
              mla_qn_g, mla_wuq, mla_kvn_g, mla_wukv,
              ffn_wg, ffn_wu, ffn_wd):
    n = x.shape[1]
    cos_da, sin_da = _axial_rope_tables(n, DA_HEAD_DIM)
    cos_mla, sin_mla = _axial_rope_tables(n, MLA_ROPE)
    x_lat, x_ctx = x, ctx
    for l in range(DEPTH):
        need_ctx = l < DEPTH - 1
        lambda_init = 0.8 - 0.6 * math.exp(-0.3 * l)
        sh1, sc1, g1, sh2, sc2, g2 = [m[:, None, :] for m in _ada_mods(c, w_ada[l], b_ada[l])]
        csh1, csc1, cg1, csh2, csc2, cg2 = _ada_mods(c_ctx, w_ada[l], b_ada[l])

        u_lat = _modulate(x_lat, sh1, sc1) @ w_in[l]
        u_ctx = _modulate(x_ctx, csh1, csc1) @ w_in[l]
        da_l, da_c = _diff_attention(u_lat[..., :OFF_LRU], u_ctx[..., :OFF_LRU], cos_da, sin_da,
                                     da_lq1[l], da_lk1[l], da_lq2[l], da_lk2[l], da_subln_g[l],
                                     lambda_init, need_ctx)
        lru_l, lru_c = _rglru_mixer(u_lat[..., OFF_LRU:OFF_MLA], u_ctx[..., OFF_LRU:OFF_MLA],
                                    lru_conv_w[l], lru_conv_b[l], lru_wr[l], lru_br[l],
                                    lru_wi[l], lru_bi[l], lru_lam[l], need_ctx)
        mla_l, mla_c = _mla(u_lat[..., OFF_MLA:OFF_FFT], u_ctx[..., OFF_MLA:OFF_FFT], cos_mla, sin_mla,
                            mla_qn_g[l], mla_wuq[l], mla_kvn_g[l], mla_wukv[l], need_ctx)
        fft_l = _fourier(u_lat[..., OFF_FFT:])
        mix_l = jnp.concatenate([da_l, lru_l, mla_l, fft_l], axis=-1).astype(x_lat.dtype)
        x_lat = _layernorm(DEEPNORM_ALPHA * x_lat + g1 * (mix_l @ w_out[l]), ln1_g[l], ln1_b[l])

        ff_l = _swiglu(_modulate(x_lat, sh2, sc2), ffn_wg[l], ffn_wu[l], ffn_wd[l])
        x_lat = _layernorm(DEEPNORM_ALPHA * x_lat + g2 * ff_l, ln2_g[l], ln2_b[l])

        if need_ctx:
            fft_c = _fourier(u_ctx[..., OFF_FFT:])
            mix_c = jnp.concatenate([da_c, lru_c, mla_c, fft_c], axis=-1).astype(x_ctx.dtype)
            x_ctx = _layernorm(DEEPNORM_ALPHA * x_ctx + cg1 * (mix_c @ w_out[l]), ln1_g[l], ln1_b[l])
            ff_c = _swiglu(_modulate(x_ctx, csh2, csc2), ffn_wg[l], ffn_wu[l], ffn_wd[l])
            x_ctx = _layernorm(DEEPNORM_ALPHA * x_ctx + cg2 * ff_c, ln2_g[l], ln2_b[l])
    return x_lat
```

```python
import functools
import math

import numpy as np
import jax
import jax.numpy as jnp
from jax import lax
from jax.experimental import pallas as pl
from jax.experimental.pallas import tpu as pltpu

F32 = jnp.float32
BF16 = jnp.bfloat16

D_MODEL = 2048
GRID_W = 64
ROPE_BASE = 10000.0
LN_EPS = 1e-5
RMS_EPS = 1e-6

HEADS = 4
DA_HEAD_DIM = 64
GROUP_WIDTH = 512
LRU_BLOCKS = 4
LRU_BLOCK = GROUP_WIDTH // LRU_BLOCKS
CONV_W = 4
CONV_LEFT = 2
LRU_C = 8.0
MLA_Q_RANK = 384
MLA_KV_RANK = 256
MLA_NOPE = 128
MLA_ROPE = 64
MLA_V = 128
FFT_GROUPS = 4
FFT_GROUP = 128
ROT_DIM = 64
ROT_HALF = ROT_DIM // 2

LANES = 128
SUBLANES = 8
SEG_Q = 0
SEG_K = 512
SEG_V = 1024
SEG_LRU = 1536
SEG_MLA = 2560
SEG_FFT = 3328
IN_PAD = 3840
MLA_SEG = 768
MLA_HEAD_PAD = 256

FFT_N1 = 128

VMEM_LIMIT = 56 * 1024 * 1024


def _cparams(*sem):
    return pltpu.CompilerParams(dimension_semantics=sem, vmem_limit_bytes=VMEM_LIMIT)


def _row_tile(n, want):
    t = min(n, want)
    assert n % t == 0, (n, t)
    return t


def _ada_kernel(c_ref, w_ref, b_ref, o_ref):
    s = c_ref[...]
    s = s * jax.nn.sigmoid(s)
    o_ref[...] = jnp.dot(s.astype(BF16), w_ref[...].astype(BF16), preferred_element_type=F32) + b_ref[...]


def _ada_mods(c8, w_ada, b_ada):
    depth, d, n6 = w_ada.shape
    tn = 1536
    return pl.pallas_call(
        _ada_kernel,
        grid=(depth, n6 // tn),
        in_specs=[
            pl.BlockSpec((SUBLANES, d), lambda l, j: (0, 0)),
            pl.BlockSpec((None, d, tn), lambda l, j: (l, 0, j)),
            pl.BlockSpec((None, 1, tn), lambda l, j: (l, 0, j)),
        ],
        out_specs=pl.BlockSpec((None, SUBLANES, tn), lambda l, j: (l, 0, j)),
        out_shape=jax.ShapeDtypeStruct((depth, SUBLANES, n6), F32),
        compiler_params=_cparams("arbitrary", "arbitrary"),
        name="ada_mods",
    )(c8, w_ada, b_ada.reshape(depth, 1, n6))


def _rope_slab(x, cos, sin_signed):
    lane = lax.broadcasted_iota(jnp.int32, x.shape, 1)
    partner = jnp.where((lane & ROT_HALF) == 0,
                        pltpu.roll(x, LANES - ROT_HALF, 1),
                        pltpu.roll(x, ROT_HALF, 1))
    return x * cos + partner * sin_signed


def _inproj_kernel(x_ref, sh_ref, sc_ref, cos_ref, sin_ref, w_ref,
                   q_ref, k_ref, v_ref, lru_ref, mla_ref, fft_ref, *, q_scale):
    xm = (x_ref[...] * (1.0 + sc_ref[...]) + sh_ref[...]).astype(BF16)
    cos = cos_ref[...]
    sin = sin_ref[...]

    def seg(a, width):
        return jnp.dot(xm, w_ref[:, a:a + width], preferred_element_type=F32)

    for h in range(HEADS):
        o = h * LANES
        q_ref[:, o:o + LANES] = (_rope_slab(seg(SEG_Q + o, LANES), cos, sin) * q_scale).astype(BF16)
        k_ref[:, o:o + LANES] = _rope_slab(seg(SEG_K + o, LANES), cos, sin).astype(BF16)
    v_ref[...] = seg(SEG_V, GROUP_WIDTH).astype(BF16)
    lru_ref[...] = seg(SEG_LRU, 2 * GROUP_WIDTH)
    mla_ref[...] = seg(SEG_MLA, MLA_SEG)
    fft_ref[...] = seg(SEG_FFT, GROUP_WIDTH).astype(BF16)


def _inproj(x, shift, scale, cos, sin, w_in_p, tm):
    n, d = x.shape
    tm = _row_tile(n, tm)
    row = lambda i: (i, 0)
    fixed = lambda i: (0, 0)
    outs = [
        jax.ShapeDtypeStruct((n, GROUP_WIDTH), BF16),
        jax.ShapeDtypeStruct((n, GROUP_WIDTH), BF16),
        jax.ShapeDtypeStruct((n, GROUP_WIDTH), BF16),
        jax.ShapeDtypeStruct((n, 2 * GROUP_WIDTH), F32),
        jax.ShapeDtypeStruct((n, MLA_SEG), F32),
        jax.ShapeDtypeStruct((n, GROUP_WIDTH), BF16),
    ]
    return pl.pallas_call(
        functools.partial(_inproj_kernel, q_scale=DA_HEAD_DIM ** -0.5),
        grid=(n // tm,),
        in_specs=[
            pl.BlockSpec((tm, d), row),
            pl.BlockSpec((1, d), fixed),
            pl.BlockSpec((1, d), fixed),
            pl.BlockSpec((tm, LANES), row),
            pl.BlockSpec((tm, LANES), row),
            pl.BlockSpec((d, IN_PAD), fixed),
        ],
        out_specs=[pl.BlockSpec((tm, o.shape[1]), row) for o in outs],
        out_shape=outs,
        compiler_params=_cparams("arbitrary"),
        name="in_proj",
    )(x, shift, scale, cos, sin, w_in_p)


def _softmax_step(q, k, v, m_ref, l_ref, acc_ref, c):
    s = lax.dot_general(q, k, (((1,), (1,)), ((), ())), preferred_element_type=F32)
    m_prev = m_ref[c]
    m_new = jnp.maximum(m_prev, jnp.max(s, axis=-1, keepdims=True))
    alpha = jnp.exp(m_prev - m_new)
    p = jnp.exp(s - m_new)
    l_ref[c] = alpha * l_ref[c] + jnp.sum(p, axis=-1, keepdims=True)
    acc_ref[c] = alpha * acc_ref[c] + jnp.dot(p.astype(BF16), v, preferred_element_type=F32)
    m_ref[c] = m_new


def _attn_init(m_ref, l_ref, acc_ref):
    m_ref[...] = jnp.full(m_ref.shape, -jnp.inf, F32)
    l_ref[...] = jnp.zeros(l_ref.shape, F32)
    acc_ref[...] = jnp.zeros(acc_ref.shape, F32)


def _da_kernel(*refs, has_prefix, lambda_init):
    if has_prefix:
        (q_ref, kc_ref, vc_ref, k_ref, v_ref, lq1_ref, lk1_ref, lq2_ref, lk2_ref, g_ref,
         o_ref, m_ref, l_ref, acc_ref) = refs
    else:
        (q_ref, k_ref, v_ref, lq1_ref, lk1_ref, lq2_ref, lk2_ref, g_ref,
         o_ref, m_ref, l_ref, acc_ref) = refs
    j = pl.program_id(2)
    q = q_ref[...]
    lane = lax.broadcasted_iota(jnp.int32, q.shape, 1)
    qs = (jnp.where(lane < DA_HEAD_DIM, q, jnp.zeros_like(q)),
          jnp.where(lane >= DA_HEAD_DIM, q, jnp.zeros_like(q)))

    @pl.when(j == 0)
    def _():
        _attn_init(m_ref, l_ref, acc_ref)
        if has_prefix:
            for c in range(2):
                _softmax_step(qs[c], kc_ref[...], vc_ref[...], m_ref, l_ref, acc_ref, c)

    for c in range(2):
        _softmax_step(qs[c], k_ref[...], v_ref[...], m_ref, l_ref, acc_ref, c)

    @pl.when(j == pl.num_programs(2) - 1)
    def _():
        lam = (jnp.exp(jnp.sum(lq1_ref[...] * lk1_ref[...], keepdims=True))
               - jnp.exp(jnp.sum(lq2_ref[...] * lk2_ref[...], keepdims=True)) + lambda_init)
        o = acc_ref[0] / l_ref[0] - lam * (acc_ref[1] / l_ref[1])
        inv = lax.rsqrt(jnp.mean(o * o, axis=-1, keepdims=True) + RMS_EPS)
        o_ref[...] = ((o * inv * g_ref[...]) * (1.0 - lambda_init)).astype(o_ref.dtype)


def _diff_attention(q, k, v, prefix, lams, g, lambda_init, tq, tk):
    nq, nk = q.shape[0], k.shape[0]
    tq, tk = _row_tile(nq, tq), _row_tile(nk, tk)
    has_prefix = prefix is not None
    qspec = pl.BlockSpec((tq, LANES), lambda h, i, j: (i, h))
    kvspec = pl.BlockSpec((tk, LANES), lambda h, i, j: (j, h))
    small = lambda w: pl.BlockSpec((1, w), lambda h, i, j: (0, 0))
    in_specs, args = [qspec], [q]
    if has_prefix:
        nc = prefix[0].shape[0]
        cspec = pl.BlockSpec((nc, LANES), lambda h, i, j: (0, h))
        in_specs += [cspec, cspec]
        args += list(prefix)
    in_specs += [kvspec, kvspec] + [small(DA_HEAD_DIM)] * 4 + [small(LANES)]
    args += [k, v] + list(lams) + [g]
    return pl.pallas_call(
        functools.partial(_da_kernel, has_prefix=has_prefix, lambda_init=lambda_init),
        grid=(HEADS, nq // tq, nk // tk),
        in_specs=in_specs,
        out_specs=qspec,
        out_shape=jax.ShapeDtypeStruct((nq, GROUP_WIDTH), BF16),
        scratch_shapes=[
            pltpu.VMEM((2, tq, 1), F32),
            pltpu.VMEM((2, tq, 1), F32),
            pltpu.VMEM((2, tq, LANES), F32),
        ],
        compiler_params=_cparams("arbitrary", "arbitrary", "arbitrary"),
        name="diff_attention",
    )(*args)


def _mla_kernel(*refs, has_prefix):
    if has_prefix:
        q_ref, kc_ref, vc_ref, k_ref, v_ref, o_ref, m_ref, l_ref, acc_ref = refs
    else:
        q_ref, k_ref, v_ref, o_ref, m_ref, l_ref, acc_ref = refs
    j = pl.program_id(2)
    q = q_ref[...]

    @pl.when(j == 0)
    def _():
        _attn_init(m_ref, l_ref, acc_ref)
        if has_prefix:
            _softmax_step(q, kc_ref[...], vc_ref[...], m_ref, l_ref, acc_ref, 0)

    _softmax_step(q, k_ref[...], v_ref[...], m_ref, l_ref, acc_ref, 0)

    @pl.when(j == pl.num_programs(2) - 1)
    def _():
        o_ref[...] = (acc_ref[0] / l_ref[0]).astype(o_ref.dtype)


def _mla_attention(q, k, v, prefix, tq, tk):
    nq, nk = q.shape[0], k.shape[0]
    tq, tk = _row_tile(nq, tq), _row_tile(nk, tk)
    has_prefix = prefix is not None
    qspec = pl.BlockSpec((tq, MLA_HEAD_PAD), lambda h, i, j: (i, h))
    kspec = pl.BlockSpec((tk, MLA_HEAD_PAD), lambda h, i, j: (j, h))
    vspec = pl.BlockSpec((tk, MLA_V), lambda h, i, j: (j, h))
    ospec = pl.BlockSpec((tq, MLA_V), lambda h, i, j: (i, h))
    in_specs, args = [qspec], [q]
    if has_prefix:
        nc = prefix[0].shape[0]
        in_specs += [pl.BlockSpec((nc, MLA_HEAD_PAD), lambda h, i, j: (0, h)),
                     pl.BlockSpec((nc, MLA_V), lambda h, i, j: (0, h))]
        args += list(prefix)
    in_specs += [kspec, vspec]
    args += [k, v]
    return pl.pallas_call(
        functools.partial(_mla_kernel, has_prefix=has_prefix),
        grid=(HEADS, nq // tq, nk // tk),
        in_specs=in_specs,
        out_specs=ospec,
        out_shape=jax.ShapeDtypeStruct((nq, GROUP_WIDTH), BF16),
        scratch_shapes=[
            pltpu.VMEM((1, tq, 1), F32),
            pltpu.VMEM((1, tq, 1), F32),
            pltpu.VMEM((1, tq, MLA_V), F32),
        ],
        compiler_params=_cparams("arbitrary", "arbitrary", "arbitrary"),
        name="mla_attention",
    )(*args)


def _rms(x, g):
    inv = lax.rsqrt(jnp.mean(x * x, axis=-1, keepdims=True) + RMS_EPS)
    return x * inv * g


def _mla_prep_kernel(u_ref, cos_ref, sin_ref, qg_ref, kvg_ref, wq_ref, wk_ref, wv_ref,
                     q_ref, k_ref, v_ref, *, scale):
    cos = cos_ref[...]
    sin = sin_ref[...]
    cq = _rms(u_ref[:, 0:MLA_Q_RANK], qg_ref[...]).astype(BF16)
    ckv = _rms(u_ref[:, MLA_Q_RANK:MLA_Q_RANK + MLA_KV_RANK], kvg_ref[...]).astype(BF16)
    k_rope = _rope_slab(u_ref[:, MLA_Q_RANK + MLA_KV_RANK:MLA_SEG], cos, sin).astype(BF16)
    for h in range(HEADS):
        o = h * MLA_HEAD_PAD
        q_nope = jnp.dot(cq, wq_ref[:, o:o + LANES], preferred_element_type=F32)
        q_rope = jnp.dot(cq, wq_ref[:, o + LANES:o + 2 * LANES], preferred_element_type=F32)
        q_ref[:, o:o + LANES] = (q_nope * scale).astype(BF16)
        q_ref[:, o + LANES:o + 2 * LANES] = (_rope_slab(q_rope, cos, sin) * scale).astype(BF16)
        k_nope = jnp.dot(ckv, wk_ref[:, h * LANES:(h + 1) * LANES], preferred_element_type=F32)
        k_ref[:, o:o + LANES] = k_nope.astype(BF16)
        k_ref[:, o + LANES:o + 2 * LANES] = k_rope
    v_ref[...] = jnp.dot(ckv, wv_ref[...], preferred_element_type=F32).astype(BF16)


def _mla_prep(u_mla, cos, sin, qn_g, kvn_g, wq_p, wk_p, wv_p, tm):
    n = u_mla.shape[0]
    tm = _row_tile(n, tm)
    row = lambda i: (i, 0)
    fixed = lambda i: (0, 0)
    outs = [
        jax.ShapeDtypeStruct((n, HEADS * MLA_HEAD_PAD), BF16),
        jax.ShapeDtypeStruct((n, HEADS * MLA_HEAD_PAD), BF16),
        jax.ShapeDtypeStruct((n, GROUP_WIDTH), BF16),
    ]
    return pl.pallas_call(
        functools.partial(_mla_prep_kernel, scale=(MLA_NOPE + MLA_ROPE) ** -0.5),
        grid=(n // tm,),
        in_specs=[
            pl.BlockSpec((tm, MLA_SEG), row),
            pl.BlockSpec((tm, LANES), row),
            pl.BlockSpec((tm, LANES), row),
            pl.BlockSpec((1, MLA_Q_RANK), fixed),
            pl.BlockSpec((1, MLA_KV_RANK), fixed),
            pl.BlockSpec(wq_p.shape, fixed),
            pl.BlockSpec(wk_p.shape, fixed),
            pl.BlockSpec(wv_p.shape, fixed),
        ],
        out_specs=[pl.BlockSpec((tm, o.shape[1]), row) for o in outs],
        out_shape=outs,
        compiler_params=_cparams("arbitrary"),
        name="mla_prep",
    )(u_mla, cos, sin, qn_g, kvn_g, wq_p, wk_p, wv_p)


def _shift_rows(x, d, halo):
    tt = x.shape[0]
    row8 = lax.broadcasted_iota(jnp.int32, (SUBLANES, x.shape[1]), 0)
    if d > 0:
        r = pltpu.roll(x, d, 0)
        f = pltpu.roll(halo, d, 0)
        first = jnp.where(row8 < d, f, r[:SUBLANES])
        return jnp.concatenate([first, r[SUBLANES:]], axis=0) if tt > SUBLANES else first
    r = pltpu.roll(x, tt + d, 0)
    f = pltpu.roll(halo, SUBLANES + d, 0)
    last = jnp.where(row8 >= SUBLANES + d, f, r[tt - SUBLANES:])
    return jnp.concatenate([r[:tt - SUBLANES], last], axis=0) if tt > SUBLANES else last


def _lru_kernel(*refs, reverse, final):
    if final:
        (x_ref, prev_ref, next_ref, cw_ref, cb_ref, wr_ref, br_ref, wi_ref, bi_ref, lam_ref, h0_ref,
         gate_ref, hf_ref, y_ref, hlast_ref, carry_ref) = refs
    else:
        (x_ref, prev_ref, next_ref, cw_ref, cb_ref, wr_ref, br_ref, wi_ref, bi_ref, lam_ref, h0_ref,
         y_ref, hlast_ref, carry_ref) = refs
    i = pl.program_id(0)
    nt = pl.num_programs(0)
    t = (nt - 1 - i) if reverse else i
    x = x_ref[...]
    tt = x.shape[0]

    @pl.when(i == 0)
    def _():
        carry_ref[...] = h0_ref[...]

    prev = prev_ref[...] * (t > 0).astype(F32)
    nxt = next_ref[...] * (t < nt - 1).astype(F32)
    cw = cw_ref[...]
    xc = (cw[0:1] * _shift_rows(x, 2, prev) + cw[1:2] * _shift_rows(x, 1, prev)
          + cw[2:3] * x + cw[3:4] * _shift_rows(x, -1, nxt)) + cb_ref[...]

    xb = xc.astype(BF16)

    def gate(w_ref, b_ref):
        z = jnp.concatenate(
            [jnp.dot(xb[:, b * LRU_BLOCK:(b + 1) * LRU_BLOCK], w_ref[b], preferred_element_type=F32)
             for b in range(LRU_BLOCKS)], axis=1)
        return jax.nn.sigmoid(z + b_ref[...])

    r = gate(wr_ref, br_ref)
    ig = gate(wi_ref, bi_ref)
    log_a = -LRU_C * r * jax.nn.softplus(-lam_ref[...])
    a = jnp.exp(log_a)
    u = jnp.sqrt(1.0 - jnp.exp(2.0 * log_a)) * (ig * xc)

    row = lax.broadcasted_iota(jnp.int32, a.shape, 0)
    big_a, big_b = a, u
    d = 1
    while d < tt:
        if reverse:
            valid = row < tt - d
            a_s = pltpu.roll(big_a, tt - d, 0)
            b_s = pltpu.roll(big_b, tt - d, 0)
        else:
            valid = row >= d
            a_s = pltpu.roll(big_a, d, 0)
            b_s = pltpu.roll(big_b, d, 0)
        big_b = jnp.where(valid, big_a * b_s + big_b, big_b)
        big_a = jnp.where(valid, big_a * a_s, big_a)
        d *= 2
    h = big_a * carry_ref[...] + big_b
    edge = h[0:1] if reverse else h[tt - 1:tt]
    carry_ref[...] = edge
    hlast_ref[...] = edge
    if final:
        y_ref[...] = ((hf_ref[...] + h) * jax.nn.gelu(gate_ref[...])).astype(y_ref.dtype)
    else:
        y_ref[...] = h


def _lru_pass(u_lru, conv_w, conv_b, wr, br, wi, bi, lam, h0, h_fwd, reverse, tt):
    n = u_lru.shape[0]
    tt = _row_tile(n, tt)
    nt = n // tt
    per8 = tt // SUBLANES
    final = h_fwd is not None
    pos = (lambda i: nt - 1 - i) if reverse else (lambda i: i)
    tile = lambda i: (pos(i), 0)
    fixed = lambda i: (0, 0)
    fixed3 = lambda i: (0, 0, 0)
    in_specs = [
        pl.BlockSpec((tt, GROUP_WIDTH), tile),
        pl.BlockSpec((SUBLANES, GROUP_WIDTH), lambda i: (jnp.maximum(pos(i) * per8 - 1, 0), 0)),
        pl.BlockSpec((SUBLANES, GROUP_WIDTH), lambda i: (jnp.minimum((pos(i) + 1) * per8, n // SUBLANES - 1), 0)),
        pl.BlockSpec((CONV_W, GROUP_WIDTH), fixed),
        pl.BlockSpec((1, GROUP_WIDTH), fixed),
        pl.BlockSpec((LRU_BLOCKS, LRU_BLOCK, LRU_BLOCK), fixed3),
        pl.BlockSpec((1, GROUP_WIDTH), fixed),
        pl.BlockSpec((LRU_BLOCKS, LRU_BLOCK, LRU_BLOCK), fixed3),
        pl.BlockSpec((1, GROUP_WIDTH), fixed),
        pl.BlockSpec((1, GROUP_WIDTH), fixed),
        pl.BlockSpec((1, GROUP_WIDTH), fixed),
    ]
    args = [u_lru, u_lru, u_lru, conv_w, conv_b, wr, br, wi, bi, lam, h0]
    if final:
        in_specs += [pl.BlockSpec((tt, GROUP_WIDTH), lambda i: (pos(i), 1)),
                     pl.BlockSpec((tt, GROUP_WIDTH), tile)]
        args += [u_lru, h_fwd]
    return pl.pallas_call(
        functools.partial(_lru_kernel, reverse=reverse, final=final),
        grid=(nt,),
        in_specs=in_specs,
        out_specs=[pl.BlockSpec((tt, GROUP_WIDTH), tile), pl.BlockSpec((1, GROUP_WIDTH), fixed)],
        out_shape=[jax.ShapeDtypeStruct((n, GROUP_WIDTH), BF16 if final else F32),
                   jax.ShapeDtypeStruct((1, GROUP_WIDTH), F32)],
        scratch_shapes=[pltpu.VMEM((1, GROUP_WIDTH), F32)],
        compiler_params=_cparams("arbitrary"),
        name="rglru_bwd" if reverse else "rglru_fwd",
    )(*args)


def _dft_tables(n):
    k = np.arange(n, dtype=np.int64)
    ang = 2.0 * np.pi * ((k[:, None] * k[None, :]) % n).astype(np.float64) / n
    return np.cos(ang), np.sin(ang)


def _fft_small_kernel(g_ref, cs_ref, cn_ref, sn_ref, o_ref, *, norm):
    g = g_ref[...]
    a_parts, b_parts = [], []
    for grp in range(FFT_GROUPS):
        gg = g[:, grp * FFT_GROUP:(grp + 1) * FFT_GROUP]
        ab = jnp.dot(gg, cs_ref[...], preferred_element_type=F32)
        a_parts.append(ab[:, :FFT_GROUP])
        b_parts.append(ab[:, FFT_GROUP:])
    a = jnp.concatenate(a_parts, axis=1).astype(BF16)
    b = jnp.concatenate(b_parts, axis=1).astype(BF16)
    y = (jnp.dot(cn_ref[...], a, preferred_element_type=F32)
         - jnp.dot(sn_ref[...], b, preferred_element_type=F32))
    o_ref[...] = (y * norm).astype(o_ref.dtype)


def _fourier_small(g):
    n = g.shape[0]
    cc, sc = _dft_tables(FFT_GROUP)
    cn, sn = _dft_tables(n)
    cs = jnp.asarray(np.concatenate([cc, sc], axis=1), BF16)
    return pl.pallas_call(
        functools.partial(_fft_small_kernel, norm=float((n * FFT_GROUP) ** -0.5)),
        out_shape=jax.ShapeDtypeStruct((n, GROUP_WIDTH), BF16),
        compiler_params=pltpu.CompilerParams(vmem_limit_bytes=VMEM_LIMIT),
        name="fourier_ctx",
    )(g, cs, jnp.asarray(cn, BF16), jnp.asarray(sn, BF16))


def _fft_stage1_kernel(f_ref, x_ref, z_ref):
    res = jnp.dot(f_ref[...], x_ref[...], preferred_element_type=F32)
    tj = z_ref.shape[2]
    res = res.reshape(2, FFT_N1, tj * GROUP_WIDTH)
    for jj in range(tj):
        z_ref[:, :, jj, :] = res[:, :, jj * GROUP_WIDTH:(jj + 1) * GROUP_WIDTH]


def _fft_stage2_kernel(z_ref, twr_ref, twi_ref, f2_ref, cs_ref, o_ref, *, norm):
    tk1 = z_ref.shape[1]
    for i in range(tk1):
        zr = z_ref[0, i]
        zi = z_ref[1, i]
        twr = jnp.concatenate([twr_ref[i]] * (GROUP_WIDTH // LANES), axis=1)
        twi = jnp.concatenate([twi_ref[i]] * (GROUP_WIDTH // LANES), axis=1)
        zz = jnp.concatenate([zr * twr - zi * twi, zr * twi + zi * twr], axis=0).astype(BF16)
        p = jnp.dot(f2_ref[...], zz, preferred_element_type=F32)
        n2 = p.shape[0] // 2
        pr = p[:n2].astype(BF16)
        pim = p[n2:].astype(BF16)
        outs = []
        for grp in range(FFT_GROUPS):
            sl = slice(grp * FFT_GROUP, (grp + 1) * FFT_GROUP)
            lhs = jnp.concatenate([pr[:, sl], pim[:, sl]], axis=1)
            outs.append(jnp.dot(lhs, cs_ref[...], preferred_element_type=F32))
        o_ref[:, i, :] = (jnp.concatenate(outs, axis=1) * norm).astype(o_ref.dtype)


def _fourier_long(g):
    n = g.shape[0]
    n1 = FFT_N1
    assert n % (n1 * SUBLANES) == 0, n
    n2 = n // n1
    c1, s1 = _dft_tables(n1)
    f1 = jnp.asarray(np.concatenate([c1, -s1], axis=0), BF16)
    tj = min(n2, SUBLANES)
    z = pl.pallas_call(
        _fft_stage1_kernel,
        grid=(n2 // tj,),
        in_specs=[pl.BlockSpec((2 * n1, n1), lambda j: (0, 0)),
                  pl.BlockSpec((n1, tj * GROUP_WIDTH), lambda j: (0, j))],
        out_specs=pl.BlockSpec((2, n1, tj, GROUP_WIDTH), lambda j: (0, 0, j, 0)),
        out_shape=jax.ShapeDtypeStruct((2, n1, n2, GROUP_WIDTH), F32),
        compiler_params=_cparams("arbitrary"),
        name="fourier_stage1",
    )(f1, g.reshape(n1, n2 * GROUP_WIDTH))

    k1 = jnp.arange(n1, dtype=F32)[:, None]
    j2 = jnp.arange(n2, dtype=F32)[None, :]
    ang = (2.0 * np.pi / n) * (k1 * j2)
    twr = jnp.broadcast_to(jnp.cos(ang)[:, :, None], (n1, n2, LANES))
    twi = jnp.broadcast_to(-jnp.sin(ang)[:, :, None], (n1, n2, LANES))
    c2, s2 = _dft_tables(n2)
    f2 = jnp.asarray(np.block([[c2, s2], [-s2, c2]]), BF16)
    cc, sc = _dft_tables(FFT_GROUP)
    cs = jnp.asarray(np.concatenate([cc, sc], axis=0), BF16)
    tk1 = SUBLANES
    out = pl.pallas_call(
        functools.partial(_fft_stage2_kernel, norm=float((n * FFT_GROUP) ** -0.5)),
        grid=(n1 // tk1,),
        in_specs=[pl.BlockSpec((2, tk1, n2, GROUP_WIDTH), lambda i: (0, i, 0, 0)),
                  pl.BlockSpec((tk1, n2, LANES), lambda i: (i, 0, 0)),
                  pl.BlockSpec((tk1, n2, LANES), lambda i: (i, 0, 0)),
                  pl.BlockSpec((2 * n2, 2 * n2), lambda i: (0, 0)),
                  pl.BlockSpec((2 * FFT_GROUP, FFT_GROUP), lambda i: (0, 0))],
        out_specs=pl.BlockSpec((n2, tk1, GROUP_WIDTH), lambda i: (0, i, 0)),
        out_shape=jax.ShapeDtypeStruct((n2, n1, GROUP_WIDTH), BF16),
        compiler_params=_cparams("arbitrary"),
        name="fourier_stage2",
    )(z, twr, twi, f2, cs)
    return out.reshape(n, GROUP_WIDTH)


def _layernorm(z, g, b):
    mu = jnp.mean(z, axis=-1, keepdims=True)
    zc = z - mu
    var = jnp.mean(zc * zc, axis=-1, keepdims=True)
    return zc * lax.rsqrt(var + LN_EPS) * g + b


def _outproj_kernel(da_ref, lru_ref, mla_ref, fft_ref, x_ref, gate_ref, w_ref, g_ref, b_ref, o_ref, *, alpha):
    y = jnp.dot(da_ref[...], w_ref[0:GROUP_WIDTH], preferred_element_type=F32)
    y += jnp.dot(lru_ref[...], w_ref[GROUP_WIDTH:2 * GROUP_WIDTH], preferred_element_type=F32)
    y += jnp.dot(mla_ref[...], w_ref[2 * GROUP_WIDTH:3 * GROUP_WIDTH], preferred_element_type=F32)
    y += jnp.dot(fft_ref[...], w_ref[3 * GROUP_WIDTH:4 * GROUP_WIDTH], preferred_element_type=F32)
    z = alpha * x_ref[...] + gate_ref[...] * y
    o_ref[...] = _layernorm(z, g_ref[...], b_ref[...])


def _outproj_ln(parts, x, gate, w_out, ln_g, ln_b, alpha, tm):
    n, d = x.shape
    tm = _row_tile(n, tm)
    row = lambda i: (i, 0)
    fixed = lambda i: (0, 0)
    return pl.pallas_call(
        functools.partial(_outproj_kernel, alpha=alpha),
        grid=(n // tm,),
        in_specs=[pl.BlockSpec((tm, GROUP_WIDTH), row)] * 4 + [
            pl.BlockSpec((tm, d), row),
            pl.BlockSpec((1, d), fixed),
            pl.BlockSpec(w_out.shape, fixed),
            pl.BlockSpec((1, d), fixed),
            pl.BlockSpec((1, d), fixed),
        ],
        out_specs=pl.BlockSpec((tm, d), row),
        out_shape=jax.ShapeDtypeStruct((n, d), F32),
        compiler_params=_cparams("arbitrary"),
        name="out_proj_ln",
    )(*parts, x, gate, w_out, ln_g, ln_b)


def _ffn_kernel(x_ref, sh_ref, sc_ref, gate_ref, wg_ref, wu_ref, wd_ref, g_ref, b_ref, o_ref,
                h_ref, acc_ref, *, alpha):
    f = pl.program_id(1)

    @pl.when(f == 0)
    def _():
        h_ref[...] = (x_ref[...] * (1.0 + sc_ref[...]) + sh_ref[...]).astype(BF16)
        acc_ref[...] = jnp.zeros(acc_ref.shape, F32)

    h = h_ref[...]
    a = jnp.dot(h, wg_ref[...], preferred_element_type=F32)
    u = jnp.dot(h, wu_ref[...], preferred_element_type=F32)
    act = (a * jax.nn.sigmoid(a) * u).astype(BF16)
    acc_ref[...] += jnp.dot(act, wd_ref[...], preferred_element_type=F32)

    @pl.when(f == pl.num_programs(1) - 1)
    def _():
        z = alpha * x_ref[...] + gate_ref[...] * acc_ref[...]
        o_ref[...] = _layernorm(z, g_ref[...], b_ref[...])


def _ffn_ln(x, shift, scale, gate, wg, wu, wd, ln_g, ln_b, alpha, tm, tf):
    n, d = x.shape
    d_ff = wg.shape[1]
    tm = _row_tile(n, tm)
    assert d_ff % tf == 0
    row = lambda i, f: (i, 0)
    fixed = lambda i, f: (0, 0)
    return pl.pallas_call(
        functools.partial(_ffn_kernel, alpha=alpha),
        grid=(n // tm, d_ff // tf),
        in_specs=[
            pl.BlockSpec((tm, d), row),
            pl.BlockSpec((1, d), fixed),
            pl.BlockSpec((1, d), fixed),
            pl.BlockSpec((1, d), fixed),
            pl.BlockSpec((d, tf), lambda i, f: (0, f)),
            pl.BlockSpec((d, tf), lambda i, f: (0, f)),
            pl.BlockSpec((tf, d), lambda i, f: (f, 0)),
            pl.BlockSpec((1, d), fixed),
            pl.BlockSpec((1, d), fixed),
        ],
        out_specs=pl.BlockSpec((tm, d), row),
        out_shape=jax.ShapeDtypeStruct((n, d), F32),
        scratch_shapes=[pltpu.VMEM((tm, d), BF16), pltpu.VMEM((tm, d), F32)],
        compiler_params=_cparams("arbitrary", "arbitrary"),
        name="ffn_ln",
    )(x, shift, scale, gate, wg, wu, wd, ln_g, ln_b)


def _pad_in_weight(w_in):
    d = w_in.shape[0]
    src_fft = 3 * GROUP_WIDTH + 2 * GROUP_WIDTH + MLA_Q_RANK + MLA_KV_RANK + MLA_ROPE
    pad = jnp.zeros((d, SEG_FFT - (SEG_MLA + MLA_Q_RANK + MLA_KV_RANK + MLA_ROPE)), w_in.dtype)
    return jnp.concatenate([w_in[:, :src_fft], pad, w_in[:, src_fft:]], axis=1).astype(BF16)


def _pad_mla_weights(w_uq, w_ukv):
    qr = w_uq.shape[0]
    wq = w_uq.reshape(qr, HEADS, MLA_NOPE + MLA_ROPE)
    wq = jnp.concatenate([wq, jnp.zeros((qr, HEADS, MLA_HEAD_PAD - MLA_NOPE - MLA_ROPE), w_uq.dtype)], axis=2)
    wkv = w_ukv.reshape(w_ukv.shape[0], HEADS, MLA_NOPE + MLA_V)
    wk = wkv[:, :, :MLA_NOPE].reshape(w_ukv.shape[0], HEADS * MLA_NOPE)
    wv = wkv[:, :, MLA_NOPE:].reshape(w_ukv.shape[0], HEADS * MLA_V)
    return (wq.reshape(qr, HEADS * MLA_HEAD_PAD).astype(BF16), wk.astype(BF16), wv.astype(BF16))


def _rope_tables(n):
    rows = n // GRID_W
    row = jnp.repeat(jnp.arange(rows, dtype=F32), GRID_W)
    col = jnp.tile(jnp.arange(GRID_W, dtype=F32), rows)
    axis_dim = ROT_DIM // 2
    inv = ROPE_BASE ** (-jnp.arange(0, axis_dim, 2, dtype=F32) / axis_dim)
    ang = jnp.concatenate([row[:, None] * inv, col[:, None] * inv], axis=-1)
    cos, sin = jnp.cos(ang), jnp.sin(ang)
    cos_slab = jnp.concatenate([cos, cos] * (LANES // ROT_DIM), axis=1)
    sin_slab = jnp.concatenate([-sin, sin] * (LANES // ROT_DIM), axis=1)
    return cos_slab, sin_slab


def kernel(x, c, ctx, c_ctx, w_ada, b_ada, w_in, w_out, ln1_g, ln1_b, ln2_g, ln2_b,
           da_lq1, da_lk1, da_lq2, da_lk2, da_subln_g,
           lru_conv_w, lru_conv_b, lru_wr, lru_br, lru_wi, lru_bi, lru_lam,
           mla_qn_g, mla_wuq, mla_kvn_g, mla_wukv,
           ffn_wg, ffn_wu, ffn_wd):
    assert x.shape[0] == 1 and c.shape[0] == 1 and ctx.shape[0] == 1
    depth = w_ada.shape[0]
    n, d = x.shape[1], x.shape[2]
    nc = ctx.shape[1]
    alpha = (2 * depth) ** 0.25
    x_lat, x_ctx = x[0], ctx[0]

    cos_l, sin_l = _rope_tables(n)
    cos_c, sin_c = jnp.ones((nc, LANES), F32), jnp.zeros((nc, LANES), F32)

    c8 = jnp.zeros((SUBLANES, d), F32).at[0].set(c[0]).at[1].set(c_ctx)
    mods = _ada_mods(c8, w_ada, b_ada)

    row2 = lambda v: v.reshape(1, -1)
    for l in range(depth):
        need_ctx = l < depth - 1
        lambda_init = 0.8 - 0.6 * math.exp(-0.3 * l)
        m_lat = [mods[l, 0:1, i * d:(i + 1) * d] for i in range(6)]
        m_ctx = [mods[l, 1:2, i * d:(i + 1) * d] for i in range(6)]

        w_in_p = _pad_in_weight(w_in[l])
        w_out_b = w_out[l].astype(BF16)
        wq_p, wk_p, wv_p = _pad_mla_weights(mla_wuq[l], mla_wukv[l])
        wg_b, wu_b, wd_b = ffn_wg[l].astype(BF16), ffn_wu[l].astype(BF16), ffn_wd[l].astype(BF16)

        ql, kl, vl, lru_l, mla_l, fft_l = _inproj(x_lat, m_lat[0], m_lat[1], cos_l, sin_l, w_in_p, 512)
        qc, kc, vc, lru_c, mla_c, fft_c = _inproj(x_ctx, m_ctx[0], m_ctx[1], cos_c, sin_c, w_in_p, 256)

        lams = [row2(da_lq1[l]), row2(da_lk1[l]), row2(da_lq2[l]), row2(da_lk2[l])]
        g_da = row2(da_subln_g[l])
        da_l = _diff_attention(ql, kl, vl, (kc, vc), lams, g_da, lambda_init, 512, 512)

        lru_args = lambda dr: (lru_conv_w[l], row2(lru_conv_b[l]), lru_wr[l, dr].astype(BF16), row2(lru_br[l, dr]),
                               lru_wi[l, dr].astype(BF16), row2(lru_bi[l, dr]), row2(lru_lam[l, dr]))
        h0 = jnp.zeros((1, GROUP_WIDTH), F32)
        hc_f, s_f = _lru_pass(lru_c, *lru_args(0), h0, None, False, 256)
        lru_yc, s_b = _lru_pass(lru_c, *lru_args(1), h0, hc_f, True, 256)
        hl_f, _ = _lru_pass(lru_l, *lru_args(0), s_f, None, False, 256)
        lru_yl, _ = _lru_pass(lru_l, *lru_args(1), s_b, hl_f, True, 256)

        mq_l, mk_l, mv_l = _mla_prep(mla_l, cos_l, sin_l, row2(mla_qn_g[l]), row2(mla_kvn_g[l]), wq_p, wk_p, wv_p, 512)
        mq_c, mk_c, mv_c = _mla_prep(mla_c, cos_c, sin_c, row2(mla_qn_g[l]), row2(mla_kvn_g[l]), wq_p, wk_p, wv_p, 256)
        mla_yl = _mla_attention(mq_l, mk_l, mv_l, (mk_c, mv_c), 512, 512)

        fft_yl = _fourier_long(fft_l)

        x_lat = _outproj_ln([da_l, lru_yl, mla_yl, fft_yl], x_lat, m_lat[2], w_out_b,
                            row2(ln1_g[l]), row2(ln1_b[l]), alpha, 512)
        x_lat = _ffn_ln(x_lat, m_lat[3], m_lat[4], m_lat[5], wg_b, wu_b, wd_b,
                        row2(ln2_g[l]), row2(ln2_b[l]), alpha, 512, 512)

        if need_ctx:
            da_c = _diff_attention(qc, kc, vc, None, lams, g_da, lambda_init, 256, 256)
            mla_yc = _mla_attention(mq_c, mk_c, mv_c, None, 256, 256)
            fft_yc = _fourier_small(fft_c)
            x_ctx = _outproj_ln([da_c, lru_yc, mla_yc, fft_yc], x_ctx, m_ctx[2], w_out_b,
                                row2(ln1_g[l]), row2(ln1_b[l]), alpha, 256)
            x_ctx = _ffn_ln(x_ctx, m_ctx[3], m_ctx[4], m_ctx[5], wg_b, wu_b, wd_b,
                            row2(ln2_g[l]), row2(ln2_b[l]), alpha, 256, 512)
    return x_lat[None]
```

```python
import functools
import math

import numpy as np
import jax
import jax.numpy as jnp
from jax import lax
from jax.experimental import pallas as pl
from jax.experimental.pallas import tpu as pltpu

F32 = jnp.float32
BF16 = jnp.bfloat16

D_MODEL = 2048
GRID_W = 64
ROPE_BASE = 10000.0
LN_EPS = 1e-5
RMS_EPS = 1e-6

HEADS = 4
DA_HEAD_DIM = 64
GROUP_WIDTH = 512
LRU_BLOCKS = 4
LRU_BLOCK = GROUP_WIDTH // LRU_BLOCKS
CONV_W = 4
CONV_LEFT = 2
LRU_C = 8.0
MLA_Q_RANK = 384
MLA_KV_RANK = 256
MLA_NOPE = 128
MLA_ROPE = 64
MLA_V = 128
FFT_GROUPS = 4
FFT_GROUP = 128
ROT_DIM = 64
ROT_HALF = ROT_DIM // 2

LANES = 128
SUBLANES = 8
SEG_Q = 0
SEG_K = 512
SEG_V = 1024
SEG_LRU = 1536
SEG_MLA = 2560
SEG_FFT = 3328
IN_PAD = 3840
MLA_SEG = 768
MLA_HEAD_PAD = 256

FFT_N1 = 128

VMEM_LIMIT = 56 * 1024 * 1024
LOG2E = math.log2(math.e)

ATTN_TQ = 512
ATTN_TK = 2048
ATTN_CHUNK = 512
SOFTMAX_SLAB = 64
ONES_ROWS = 16


def _cparams(*sem):
    return pltpu.CompilerParams(dimension_semantics=sem, vmem_limit_bytes=VMEM_LIMIT)


def _row_tile(n, want):
    t = min(n, want)
    assert n % t == 0, (n, t)
    return t


def _ada_kernel(c_ref, w_ref, b_ref, o_ref):
    s = c_ref[...]
    s = s * jax.nn.sigmoid(s)
    o_ref[...] = jnp.dot(s.astype(BF16), w_ref[...].astype(BF16), preferred_element_type=F32) + b_ref[...]


def _ada_mods(c8, w_ada, b_ada):
    depth, d, n6 = w_ada.shape
    tn = 1536
    return pl.pallas_call(
        _ada_kernel,
        grid=(depth, n6 // tn),
        in_specs=[
            pl.BlockSpec((SUBLANES, d), lambda l, j: (0, 0)),
            pl.BlockSpec((None, d, tn), lambda l, j: (l, 0, j)),
            pl.BlockSpec((None, 1, tn), lambda l, j: (l, 0, j)),
        ],
        out_specs=pl.BlockSpec((None, SUBLANES, tn), lambda l, j: (l, 0, j)),
        out_shape=jax.ShapeDtypeStruct((depth, SUBLANES, n6), F32),
        compiler_params=_cparams("arbitrary", "arbitrary"),
        name="ada_mods",
    )(c8, w_ada, b_ada.reshape(depth, 1, n6))


def _rope_slab(x, cos, sin_signed):
    lane = lax.broadcasted_iota(jnp.int32, x.shape, 1)
    partner = jnp.where((lane & ROT_HALF) == 0,
                        pltpu.roll(x, LANES - ROT_HALF, 1),
                        pltpu.roll(x, ROT_HALF, 1))
    return x * cos + partner * sin_signed


def _inproj_kernel(x_ref, sh_ref, sc_ref, cos_ref, sin_ref, w_ref,
                   q_ref, k_ref, v_ref, lru_ref, mla_ref, fft_ref, *, q_scale):
    xm = (x_ref[...] * (1.0 + sc_ref[...]) + sh_ref[...]).astype(BF16)
    cos = cos_ref[...]
    sin = sin_ref[...]

    def seg(a, width):
        return jnp.dot(xm, w_ref[:, a:a + width], preferred_element_type=F32)

    for h in range(HEADS):
        o = h * LANES
        q_ref[:, o:o + LANES] = (_rope_slab(seg(SEG_Q + o, LANES), cos, sin) * q_scale).astype(BF16)
        k_ref[:, o:o + LANES] = _rope_slab(seg(SEG_K + o, LANES), cos, sin).astype(BF16)
    v_ref[...] = seg(SEG_V, GROUP_WIDTH).T.astype(BF16)
    lru_ref[...] = seg(SEG_LRU, 2 * GROUP_WIDTH)
    mla_ref[...] = seg(SEG_MLA, MLA_SEG)
    fft_ref[...] = seg(SEG_FFT, GROUP_WIDTH).astype(BF16)


def _inproj(x, shift, scale, cos, sin, w_in_p, tm):
    n, d = x.shape
    tm = _row_tile(n, tm)
    row = lambda i: (i, 0)
    col = lambda i: (0, i)
    fixed = lambda i: (0, 0)
    outs = [
        jax.ShapeDtypeStruct((n, GROUP_WIDTH), BF16),
        jax.ShapeDtypeStruct((n, GROUP_WIDTH), BF16),
        jax.ShapeDtypeStruct((GROUP_WIDTH, n), BF16),
        jax.ShapeDtypeStruct((n, 2 * GROUP_WIDTH), F32),
        jax.ShapeDtypeStruct((n, MLA_SEG), F32),
        jax.ShapeDtypeStruct((n, GROUP_WIDTH), BF16),
    ]
    out_specs = [pl.BlockSpec((tm, o.shape[1]), row) for o in outs]
    out_specs[2] = pl.BlockSpec((GROUP_WIDTH, tm), col)
    return pl.pallas_call(
        functools.partial(_inproj_kernel, q_scale=DA_HEAD_DIM ** -0.5 * LOG2E),
        grid=(n // tm,),
        in_specs=[
            pl.BlockSpec((tm, d), row),
            pl.BlockSpec((1, d), fixed),
            pl.BlockSpec((1, d), fixed),
            pl.BlockSpec((tm, LANES), row),
            pl.BlockSpec((tm, LANES), row),
            pl.BlockSpec((d, IN_PAD), fixed),
        ],
        out_specs=out_specs,
        out_shape=outs,
        compiler_params=_cparams("arbitrary"),
        name="in_proj",
    )(x, shift, scale, cos, sin, w_in_p)


def _scores_t(q_t, k):
    return jnp.dot(k, q_t, preferred_element_type=F32)


def _softmax_update(s_t, v_t, m_ref, acc_ref, c):
    keys, tq = s_t.shape
    m_prev = m_ref[c]
    m_new = jnp.maximum(m_prev, jnp.max(s_t, axis=0, keepdims=True))
    alpha = jnp.exp2(m_prev - m_new)
    slab = min(keys, SOFTMAX_SLAB)
    p_t = jnp.concatenate([jnp.exp2(s_t[r:r + slab] - m_new).astype(BF16) for r in range(0, keys, slab)], axis=0)
    v_ext = jnp.concatenate([v_t, jnp.ones((ONES_ROWS, keys), BF16)], axis=0)
    acc_ref[c] = alpha * acc_ref[c] + jnp.dot(v_ext, p_t, preferred_element_type=F32)
    m_ref[c] = m_new


def _flash_kernel(*refs, n_comp, has_prefix, chunk, epilogue):
    refs = list(refs)
    q_ref = refs.pop(0)
    if has_prefix:
        kc_ref, vtc_ref = refs.pop(0), refs.pop(0)
    k_ref, vt_ref = refs.pop(0), refs.pop(0)
    qt_ref, m_ref, acc_ref = refs[-3:]
    o_ref = refs[-4]
    extra = refs[:-4]
    j = pl.program_id(2)

    @pl.when(j == 0)
    def _():
        m_ref[...] = jnp.full(m_ref.shape, -jnp.inf, F32)
        acc_ref[...] = jnp.zeros(acc_ref.shape, F32)
        q_t = q_ref[...].astype(F32).T
        if n_comp == 2:
            row = lax.broadcasted_iota(jnp.int32, q_t.shape, 0)
            qt_ref[0] = jnp.where(row < DA_HEAD_DIM, q_t, 0.0).astype(BF16)
            qt_ref[1] = jnp.where(row >= DA_HEAD_DIM, q_t, 0.0).astype(BF16)
        else:
            qt_ref[0] = q_t.astype(BF16)
        if has_prefix:
            for c in range(n_comp):
                _softmax_update(_scores_t(qt_ref[c], kc_ref[...]), vtc_ref[...], m_ref, acc_ref, c)

    units = [(i * chunk, c) for i in range(k_ref.shape[0] // chunk) for c in range(n_comp)]
    scores = lambda u: _scores_t(qt_ref[units[u][1]], k_ref[units[u][0]:units[u][0] + chunk, :])
    s_next = scores(0)
    for u, (off, c) in enumerate(units):
        s_cur = s_next
        if u + 1 < len(units):
            s_next = scores(u + 1)
        _softmax_update(s_cur, vt_ref[:, off:off + chunk], m_ref, acc_ref, c)

    @pl.when(j == pl.num_programs(2) - 1)
    def _():
        epilogue(extra, o_ref, acc_ref)


def _normalised_t(acc_ref, c):
    dv = acc_ref.shape[1] - ONES_ROWS
    return acc_ref[c, 0:dv, :] / acc_ref[c, dv:dv + 1, :]


def _da_epilogue(extra, o_ref, acc_ref, *, lambda_init):
    lq1_ref, lk1_ref, lq2_ref, lk2_ref, g_ref = extra
    lam = (jnp.exp(jnp.sum(lq1_ref[...] * lk1_ref[...], keepdims=True))
           - jnp.exp(jnp.sum(lq2_ref[...] * lk2_ref[...], keepdims=True)) + lambda_init)
    o = (_normalised_t(acc_ref, 0) - lam * _normalised_t(acc_ref, 1)).T
    inv = lax.rsqrt(jnp.mean(o * o, axis=-1, keepdims=True) + RMS_EPS)
    o_ref[...] = ((o * inv * g_ref[...]) * (1.0 - lambda_init)).astype(o_ref.dtype)


def _mla_epilogue(extra, o_ref, acc_ref):
    o_ref[...] = _normalised_t(acc_ref, 0).T.astype(o_ref.dtype)


def _flash_attention(q, k, v_t, prefix, extra, extra_specs, *, dk, n_comp, epilogue, tq, tk, chunk, name):
    nq, nk = q.shape[0], k.shape[0]
    dv = v_t.shape[0] // HEADS
    tq, tk = _row_tile(nq, tq), _row_tile(nk, tk)
    chunk = _row_tile(tk, chunk)
    has_prefix = prefix is not None
    qspec = pl.BlockSpec((tq, dk), lambda h, i, j: (i, h))
    in_specs, args = [qspec], [q]
    if has_prefix:
        nc = prefix[0].shape[0]
        in_specs += [pl.BlockSpec((nc, dk), lambda h, i, j: (0, h)),
                     pl.BlockSpec((dv, nc), lambda h, i, j: (h, 0))]
        args += list(prefix)
    in_specs += [pl.BlockSpec((tk, dk), lambda h, i, j: (j, h)),
                 pl.BlockSpec((dv, tk), lambda h, i, j: (h, j))] + list(extra_specs)
    args += [k, v_t] + list(extra)
    return pl.pallas_call(
        functools.partial(_flash_kernel, n_comp=n_comp, has_prefix=has_prefix, chunk=chunk, epilogue=epilogue),
        grid=(HEADS, nq // tq, nk // tk),
        in_specs=in_specs,
        out_specs=pl.BlockSpec((tq, dv), lambda h, i, j: (i, h)),
        out_shape=jax.ShapeDtypeStruct((nq, HEADS * dv), BF16),
        scratch_shapes=[
            pltpu.VMEM((n_comp, dk, tq), BF16),
            pltpu.VMEM((n_comp, 1, tq), F32),
            pltpu.VMEM((n_comp, dv + ONES_ROWS, tq), F32),
        ],
        compiler_params=_cparams("arbitrary", "arbitrary", "arbitrary"),
        name=name,
    )(*args)


def _diff_attention(q, k, v_t, prefix, lams, g, lambda_init, tq, tk, chunk):
    small = lambda w: pl.BlockSpec((1, w), lambda h, i, j: (0, 0))
    return _flash_attention(
        q, k, v_t, prefix, list(lams) + [g], [small(DA_HEAD_DIM)] * 4 + [small(LANES)],
        dk=2 * DA_HEAD_DIM, n_comp=2, epilogue=functools.partial(_da_epilogue, lambda_init=lambda_init),
        tq=tq, tk=tk, chunk=chunk, name="diff_attention")


def _mla_attention(q, k, v_t, prefix, tq, tk, chunk):
    return _flash_attention(q, k, v_t, prefix, [], [], dk=MLA_HEAD_PAD, n_comp=1, epilogue=_mla_epilogue,
                            tq=tq, tk=tk, chunk=chunk, name="mla_attention")


def _rms(x, g):
    inv = lax.rsqrt(jnp.mean(x * x, axis=-1, keepdims=True) + RMS_EPS)
    return x * inv * g


def _mla_prep_kernel(u_ref, cos_ref, sin_ref, qg_ref, kvg_ref, wq_ref, wk_ref, wv_ref,
                     q_ref, k_ref, v_ref, *, scale):
    cos = cos_ref[...]
    sin = sin_ref[...]
    cq = _rms(u_ref[:, 0:MLA_Q_RANK], qg_ref[...]).astype(BF16)
    ckv = _rms(u_ref[:, MLA_Q_RANK:MLA_Q_RANK + MLA_KV_RANK], kvg_ref[...]).astype(BF16)
    k_rope = _rope_slab(u_ref[:, MLA_Q_RANK + MLA_KV_RANK:MLA_SEG], cos, sin).astype(BF16)
    for h in range(HEADS):
        o = h * MLA_HEAD_PAD
        q_nope = jnp.dot(cq, wq_ref[:, o:o + LANES], preferred_element_type=F32)
        q_rope = jnp.dot(cq, wq_ref[:, o + LANES:o + 2 * LANES], preferred_element_type=F32)
        q_ref[:, o:o + LANES] = (q_nope * scale).astype(BF16)
        q_ref[:, o + LANES:o + 2 * LANES] = (_rope_slab(q_rope, cos, sin) * scale).astype(BF16)
        k_nope = jnp.dot(ckv, wk_ref[:, h * LANES:(h + 1) * LANES], preferred_element_type=F32)
        k_ref[:, o:o + LANES] = k_nope.astype(BF16)
        k_ref[:, o + LANES:o + 2 * LANES] = k_rope
    v_ref[...] = jnp.dot(ckv, wv_ref[...], preferred_element_type=F32).T.astype(BF16)


def _mla_prep(u_mla, cos, sin, qn_g, kvn_g, wq_p, wk_p, wv_p, tm):
    n = u_mla.shape[0]
    tm = _row_tile(n, tm)
    row = lambda i: (i, 0)
    fixed = lambda i: (0, 0)
    outs = [
        jax.ShapeDtypeStruct((n, HEADS * MLA_HEAD_PAD), BF16),
        jax.ShapeDtypeStruct((n, HEADS * MLA_HEAD_PAD), BF16),
        jax.ShapeDtypeStruct((GROUP_WIDTH, n), BF16),
    ]
    out_specs = [pl.BlockSpec((tm, HEADS * MLA_HEAD_PAD), row)] * 2 + [pl.BlockSpec((GROUP_WIDTH, tm), lambda i: (0, i))]
    return pl.pallas_call(
        functools.partial(_mla_prep_kernel, scale=(MLA_NOPE + MLA_ROPE) ** -0.5 * LOG2E),
        grid=(n // tm,),
        in_specs=[
            pl.BlockSpec((tm, MLA_SEG), row),
            pl.BlockSpec((tm, LANES), row),
            pl.BlockSpec((tm, LANES), row),
            pl.BlockSpec((1, MLA_Q_RANK), fixed),
            pl.BlockSpec((1, MLA_KV_RANK), fixed),
            pl.BlockSpec(wq_p.shape, fixed),
            pl.BlockSpec(wk_p.shape, fixed),
            pl.BlockSpec(wv_p.shape, fixed),
        ],
        out_specs=out_specs,
        out_shape=outs,
        compiler_params=_cparams("arbitrary"),
        name="mla_prep",
    )(u_mla, cos, sin, qn_g, kvn_g, wq_p, wk_p, wv_p)


def _shift_rows(x, d, halo):
    tt = x.shape[0]
    row8 = lax.broadcasted_iota(jnp.int32, (SUBLANES, x.shape[1]), 0)
    if d > 0:
        r = pltpu.roll(x, d, 0)
        f = pltpu.roll(halo, d, 0)
        first = jnp.where(row8 < d, f, r[:SUBLANES])
        return jnp.concatenate([first, r[SUBLANES:]], axis=0) if tt > SUBLANES else first
    r = pltpu.roll(x, tt + d, 0)
    f = pltpu.roll(halo, SUBLANES + d, 0)
    last = jnp.where(row8 >= SUBLANES + d, f, r[tt - SUBLANES:])
    return jnp.concatenate([r[:tt - SUBLANES], last], axis=0) if tt > SUBLANES else last


def _lru_kernel(*refs, reverse, final):
    if final:
        (x_ref, prev_ref, next_ref, cw_ref, cb_ref, wr_ref, br_ref, wi_ref, bi_ref, lam_ref, h0_ref,
         gate_ref, hf_ref, y_ref, hlast_ref, carry_ref) = refs
    else:
        (x_ref, prev_ref, next_ref, cw_ref, cb_ref, wr_ref, br_ref, wi_ref, bi_ref, lam_ref, h0_ref,
         y_ref, hlast_ref, carry_ref) = refs
    i = pl.program_id(0)
    nt = pl.num_programs(0)
    t = (nt - 1 - i) if reverse else i
    x = x_ref[...]
    tt = x.shape[0]

    @pl.when(i == 0)
    def _():
        carry_ref[...] = h0_ref[...]

    prev = prev_ref[...] * (t > 0).astype(F32)
    nxt = next_ref[...] * (t < nt - 1).astype(F32)
    cw = cw_ref[...]
    xc = (cw[0:1] * _shift_rows(x, 2, prev) + cw[1:2] * _shift_rows(x, 1, prev)
          + cw[2:3] * x + cw[3:4] * _shift_rows(x, -1, nxt)) + cb_ref[...]

    xb = xc.astype(BF16)

    def gate(w_ref, b_ref):
        z = jnp.concatenate(
            [jnp.dot(xb[:, b * LRU_BLOCK:(b + 1) * LRU_BLOCK], w_ref[b], preferred_element_type=F32)
             for b in range(LRU_BLOCKS)], axis=1)
        return jax.nn.sigmoid(z + b_ref[...])

    r = gate(wr_ref, br_ref)
    ig = gate(wi_ref, bi_ref)
    log_a = -LRU_C * r * jax.nn.softplus(-lam_ref[...])
    a = jnp.exp(log_a)
    u = jnp.sqrt(1.0 - jnp.exp(2.0 * log_a)) * (ig * xc)

    row = lax.broadcasted_iota(jnp.int32, a.shape, 0)
    big_a, big_b = a, u
    d = 1
    while d < tt:
        if reverse:
            valid = row < tt - d
            a_s = pltpu.roll(big_a, tt - d, 0)
            b_s = pltpu.roll(big_b, tt - d, 0)
        else:
            valid = row >= d
            a_s = pltpu.roll(big_a, d, 0)
            b_s = pltpu.roll(big_b, d, 0)
        big_b = jnp.where(valid, big_a * b_s + big_b, big_b)
        big_a = jnp.where(valid, big_a * a_s, big_a)
        d *= 2
    h = big_a * carry_ref[...] + big_b
    edge = h[0:1] if reverse else h[tt - 1:tt]
    carry_ref[...] = edge
    hlast_ref[...] = edge
    if final:
        y_ref[...] = ((hf_ref[...] + h) * jax.nn.gelu(gate_ref[...])).astype(y_ref.dtype)
    else:
        y_ref[...] = h


def _lru_pass(u_lru, conv_w, conv_b, wr, br, wi, bi, lam, h0, h_fwd, reverse, tt):
    n = u_lru.shape[0]
    tt = _row_tile(n, tt)
    nt = n // tt
    per8 = tt // SUBLANES
    final = h_fwd is not None
    pos = (lambda i: nt - 1 - i) if reverse else (lambda i: i)
    tile = lambda i: (pos(i), 0)
    fixed = lambda i: (0, 0)
    fixed3 = lambda i: (0, 0, 0)
    in_specs = [
        pl.BlockSpec((tt, GROUP_WIDTH), tile),
        pl.BlockSpec((SUBLANES, GROUP_WIDTH), lambda i: (jnp.maximum(pos(i) * per8 - 1, 0), 0)),
        pl.BlockSpec((SUBLANES, GROUP_WIDTH), lambda i: (jnp.minimum((pos(i) + 1) * per8, n // SUBLANES - 1), 0)),
        pl.BlockSpec((CONV_W, GROUP_WIDTH), fixed),
        pl.BlockSpec((1, GROUP_WIDTH), fixed),
        pl.BlockSpec((LRU_BLOCKS, LRU_BLOCK, LRU_BLOCK), fixed3),
        pl.BlockSpec((1, GROUP_WIDTH), fixed),
        pl.BlockSpec((LRU_BLOCKS, LRU_BLOCK, LRU_BLOCK), fixed3),
        pl.BlockSpec((1, GROUP_WIDTH), fixed),
        pl.BlockSpec((1, GROUP_WIDTH), fixed),
        pl.BlockSpec((1, GROUP_WIDTH), fixed),
    ]
    args = [u_lru, u_lru, u_lru, conv_w, conv_b, wr, br, wi, bi, lam, h0]
    if final:
        in_specs += [pl.BlockSpec((tt, GROUP_WIDTH), lambda i: (pos(i), 1)),
                     pl.BlockSpec((tt, GROUP_WIDTH), tile)]
        args += [u_lru, h_fwd]
    return pl.pallas_call(
        functools.partial(_lru_kernel, reverse=reverse, final=final),
        grid=(nt,),
        in_specs=in_specs,
        out_specs=[pl.BlockSpec((tt, GROUP_WIDTH), tile), pl.BlockSpec((1, GROUP_WIDTH), fixed)],
        out_shape=[jax.ShapeDtypeStruct((n, GROUP_WIDTH), BF16 if final else F32),
                   jax.ShapeDtypeStruct((1, GROUP_WIDTH), F32)],
        scratch_shapes=[pltpu.VMEM((1, GROUP_WIDTH), F32)],
        compiler_params=_cparams("arbitrary"),
        name="rglru_bwd" if reverse else "rglru_fwd",
    )(*args)


def _dft_tables(n):
    k = np.arange(n, dtype=np.int64)
    ang = 2.0 * np.pi * ((k[:, None] * k[None, :]) % n).astype(np.float64) / n
    return np.cos(ang), np.sin(ang)


def _fft_small_kernel(g_ref, cs_ref, cn_ref, sn_ref, o_ref, *, norm):
    g = g_ref[...]
    a_parts, b_parts = [], []
    for grp in range(FFT_GROUPS):
        gg = g[:, grp * FFT_GROUP:(grp + 1) * FFT_GROUP]
        ab = jnp.dot(gg, cs_ref[...], preferred_element_type=F32)
        a_parts.append(ab[:, :FFT_GROUP])
        b_parts.append(ab[:, FFT_GROUP:])
    a = jnp.concatenate(a_parts, axis=1).astype(BF16)
    b = jnp.concatenate(b_parts, axis=1).astype(BF16)
    y = (jnp.dot(cn_ref[...], a, preferred_element_type=F32)
         - jnp.dot(sn_ref[...], b, preferred_element_type=F32))
    o_ref[...] = (y * norm).astype(o_ref.dtype)


def _fourier_small(g):
    n = g.shape[0]
    cc, sc = _dft_tables(FFT_GROUP)
    cn, sn = _dft_tables(n)
    cs = jnp.asarray(np.concatenate([cc, sc], axis=1), BF16)
    return pl.pallas_call(
        functools.partial(_fft_small_kernel, norm=float((n * FFT_GROUP) ** -0.5)),
        out_shape=jax.ShapeDtypeStruct((n, GROUP_WIDTH), BF16),
        compiler_params=pltpu.CompilerParams(vmem_limit_bytes=VMEM_LIMIT),
        name="fourier_ctx",
    )(g, cs, jnp.asarray(cn, BF16), jnp.asarray(sn, BF16))


def _fft_stage1_kernel(f_ref, x_ref, z_ref):
    res = jnp.dot(f_ref[...], x_ref[...], preferred_element_type=F32)
    tj = z_ref.shape[2]
    res = res.reshape(2, FFT_N1, tj * GROUP_WIDTH)
    for jj in range(tj):
        z_ref[:, :, jj, :] = res[:, :, jj * GROUP_WIDTH:(jj + 1) * GROUP_WIDTH]


def _fft_stage2_kernel(z_ref, twr_ref, twi_ref, f2_ref, cs_ref, o_ref, *, norm):
    tk1 = z_ref.shape[1]
    for i in range(tk1):
        zr = z_ref[0, i]
        zi = z_ref[1, i]
        twr = jnp.concatenate([twr_ref[i]] * (GROUP_WIDTH // LANES), axis=1)
        twi = jnp.concatenate([twi_ref[i]] * (GROUP_WIDTH // LANES), axis=1)
        zz = jnp.concatenate([zr * twr - zi * twi, zr * twi + zi * twr], axis=0).astype(BF16)
        p = jnp.dot(f2_ref[...], zz, preferred_element_type=F32)
        n2 = p.shape[0] // 2
        pr = p[:n2].astype(BF16)
        pim = p[n2:].astype(BF16)
        outs = []
        for grp in range(FFT_GROUPS):
            sl = slice(grp * FFT_GROUP, (grp + 1) * FFT_GROUP)
            lhs = jnp.concatenate([pr[:, sl], pim[:, sl]], axis=1)
            outs.append(jnp.dot(lhs, cs_ref[...], preferred_element_type=F32))
        o_ref[:, i, :] = (jnp.concatenate(outs, axis=1) * norm).astype(o_ref.dtype)


def _fourier_long(g):
    n = g.shape[0]
    n1 = FFT_N1
    assert n % (n1 * SUBLANES) == 0, n
    n2 = n // n1
    c1, s1 = _dft_tables(n1)
    f1 = jnp.asarray(np.concatenate([c1, -s1], axis=0), BF16)
    tj = min(n2, SUBLANES)
    z = pl.pallas_call(
        _fft_stage1_kernel,
        grid=(n2 // tj,),
        in_specs=[pl.BlockSpec((2 * n1, n1), lambda j: (0, 0)),
                  pl.BlockSpec((n1, tj * GROUP_WIDTH), lambda j: (0, j))],
        out_specs=pl.BlockSpec((2, n1, tj, GROUP_WIDTH), lambda j: (0, 0, j, 0)),
        out_shape=jax.ShapeDtypeStruct((2, n1, n2, GROUP_WIDTH), F32),
        compiler_params=_cparams("arbitrary"),
        name="fourier_stage1",
    )(f1, g.reshape(n1, n2 * GROUP_WIDTH))

    k1 = jnp.arange(n1, dtype=F32)[:, None]
    j2 = jnp.arange(n2, dtype=F32)[None, :]
    ang = (2.0 * np.pi / n) * (k1 * j2)
    twr = jnp.broadcast_to(jnp.cos(ang)[:, :, None], (n1, n2, LANES))
    twi = jnp.broadcast_to(-jnp.sin(ang)[:, :, None], (n1, n2, LANES))
    c2, s2 = _dft_tables(n2)
    f2 = jnp.asarray(np.block([[c2, s2], [-s2, c2]]), BF16)
    cc, sc = _dft_tables(FFT_GROUP)
    cs = jnp.asarray(np.concatenate([cc, sc], axis=0), BF16)
    tk1 = SUBLANES
    out = pl.pallas_call(
        functools.partial(_fft_stage2_kernel, norm=float((n * FFT_GROUP) ** -0.5)),
        grid=(n1 // tk1,),
        in_specs=[pl.BlockSpec((2, tk1, n2, GROUP_WIDTH), lambda i: (0, i, 0, 0)),
                  pl.BlockSpec((tk1, n2, LANES), lambda i: (i, 0, 0)),
                  pl.BlockSpec((tk1, n2, LANES), lambda i: (i, 0, 0)),
                  pl.BlockSpec((2 * n2, 2 * n2), lambda i: (0, 0)),
                  pl.BlockSpec((2 * FFT_GROUP, FFT_GROUP), lambda i: (0, 0))],
        out_specs=pl.BlockSpec((n2, tk1, GROUP_WIDTH), lambda i: (0, i, 0)),
        out_shape=jax.ShapeDtypeStruct((n2, n1, GROUP_WIDTH), BF16),
        compiler_params=_cparams("arbitrary"),
        name="fourier_stage2",
    )(z, twr, twi, f2, cs)
    return out.reshape(n, GROUP_WIDTH)


def _layernorm(z, g, b):
    mu = jnp.mean(z, axis=-1, keepdims=True)
    zc = z - mu
    var = jnp.mean(zc * zc, axis=-1, keepdims=True)
    return zc * lax.rsqrt(var + LN_EPS) * g + b


def _outproj_kernel(da_ref, lru_ref, mla_ref, fft_ref, x_ref, gate_ref, w_ref, g_ref, b_ref, o_ref, *, alpha):
    y = jnp.dot(da_ref[...], w_ref[0:GROUP_WIDTH], preferred_element_type=F32)
    y += jnp.dot(lru_ref[...], w_ref[GROUP_WIDTH:2 * GROUP_WIDTH], preferred_element_type=F32)
    y += jnp.dot(mla_ref[...], w_ref[2 * GROUP_WIDTH:3 * GROUP_WIDTH], preferred_element_type=F32)
    y += jnp.dot(fft_ref[...], w_ref[3 * GROUP_WIDTH:4 * GROUP_WIDTH], preferred_element_type=F32)
    z = alpha * x_ref[...] + gate_ref[...] * y
    o_ref[...] = _layernorm(z, g_ref[...], b_ref[...])


def _outproj_ln(parts, x, gate, w_out, ln_g, ln_b, alpha, tm):
    n, d = x.shape
    tm = _row_tile(n, tm)
    row = lambda i: (i, 0)
    fixed = lambda i: (0, 0)
    return pl.pallas_call(
        functools.partial(_outproj_kernel, alpha=alpha),
        grid=(n // tm,),
        in_specs=[pl.BlockSpec((tm, GROUP_WIDTH), row)] * 4 + [
            pl.BlockSpec((tm, d), row),
            pl.BlockSpec((1, d), fixed),
            pl.BlockSpec(w_out.shape, fixed),
            pl.BlockSpec((1, d), fixed),
            pl.BlockSpec((1, d), fixed),
        ],
        out_specs=pl.BlockSpec((tm, d), row),
        out_shape=jax.ShapeDtypeStruct((n, d), F32),
        compiler_params=_cparams("arbitrary"),
        name="out_proj_ln",
    )(*parts, x, gate, w_out, ln_g, ln_b)


def _ffn_kernel(x_ref, sh_ref, sc_ref, gate_ref, wg_ref, wu_ref, wd_ref, g_ref, b_ref, o_ref,
                h_ref, acc_ref, *, alpha):
    f = pl.program_id(1)

    @pl.when(f == 0)
    def _():
        h_ref[...] = (x_ref[...] * (1.0 + sc_ref[...]) + sh_ref[...]).astype(BF16)
        acc_ref[...] = jnp.zeros(acc_ref.shape, F32)

    h = h_ref[...]
    a = jnp.dot(h, wg_ref[...], preferred_element_type=F32)
    u = jnp.dot(h, wu_ref[...], preferred_element_type=F32)
    act = (a * jax.nn.sigmoid(a) * u).astype(BF16)
    acc_ref[...] += jnp.dot(act, wd_ref[...], preferred_element_type=F32)

    @pl.when(f == pl.num_programs(1) - 1)
    def _():
        z = alpha * x_ref[...] + gate_ref[...] * acc_ref[...]
        o_ref[...] = _layernorm(z, g_ref[...], b_ref[...])


def _ffn_ln(x, shift, scale, gate, wg, wu, wd, ln_g, ln_b, alpha, tm, tf):
    n, d = x.shape
    d_ff = wg.shape[1]
    tm = _row_tile(n, tm)
    assert d_ff % tf == 0
    row = lambda i, f: (i, 0)
    fixed = lambda i, f: (0, 0)
    return pl.pallas_call(
        functools.partial(_ffn_kernel, alpha=alpha),
        grid=(n // tm, d_ff // tf),
        in_specs=[
            pl.BlockSpec((tm, d), row),
            pl.BlockSpec((1, d), fixed),
            pl.BlockSpec((1, d), fixed),
            pl.BlockSpec((1, d), fixed),
            pl.BlockSpec((d, tf), lambda i, f: (0, f)),
            pl.BlockSpec((d, tf), lambda i, f: (0, f)),
            pl.BlockSpec((tf, d), lambda i, f: (f, 0)),
            pl.BlockSpec((1, d), fixed),
            pl.BlockSpec((1, d), fixed),
        ],
        out_specs=pl.BlockSpec((tm, d), row),
        out_shape=jax.ShapeDtypeStruct((n, d), F32),
        scratch_shapes=[pltpu.VMEM((tm, d), BF16), pltpu.VMEM((tm, d), F32)],
        compiler_params=_cparams("arbitrary", "arbitrary"),
        name="ffn_ln",
    )(x, shift, scale, gate, wg, wu, wd, ln_g, ln_b)


def _pad_in_weight(w_in):
    d = w_in.shape[0]
    src_fft = 3 * GROUP_WIDTH + 2 * GROUP_WIDTH + MLA_Q_RANK + MLA_KV_RANK + MLA_ROPE
    pad = jnp.zeros((d, SEG_FFT - (SEG_MLA + MLA_Q_RANK + MLA_KV_RANK + MLA_ROPE)), w_in.dtype)
    return jnp.concatenate([w_in[:, :src_fft], pad, w_in[:, src_fft:]], axis=1).astype(BF16)


def _pad_mla_weights(w_uq, w_ukv):
    qr = w_uq.shape[0]
    wq = w_uq.reshape(qr, HEADS, MLA_NOPE + MLA_ROPE)
    wq = jnp.concatenate([wq, jnp.zeros((qr, HEADS, MLA_HEAD_PAD - MLA_NOPE - MLA_ROPE), w_uq.dtype)], axis=2)
    wkv = w_ukv.reshape(w_ukv.shape[0], HEADS, MLA_NOPE + MLA_V)
    wk = wkv[:, :, :MLA_NOPE].reshape(w_ukv.shape[0], HEADS * MLA_NOPE)
    wv = wkv[:, :, MLA_NOPE:].reshape(w_ukv.shape[0], HEADS * MLA_V)
    return (wq.reshape(qr, HEADS * MLA_HEAD_PAD).astype(BF16), wk.astype(BF16), wv.astype(BF16))


def _rope_tables(n):
    rows = n // GRID_W
    row = jnp.repeat(jnp.arange(rows, dtype=F32), GRID_W)
    col = jnp.tile(jnp.arange(GRID_W, dtype=F32), rows)
    axis_dim = ROT_DIM // 2
    inv = ROPE_BASE ** (-jnp.arange(0, axis_dim, 2, dtype=F32) / axis_dim)
    ang = jnp.concatenate([row[:, None] * inv, col[:, None] * inv], axis=-1)
    cos, sin = jnp.cos(ang), jnp.sin(ang)
    cos_slab = jnp.concatenate([cos, cos] * (LANES // ROT_DIM), axis=1)
    sin_slab = jnp.concatenate([-sin, sin] * (LANES // ROT_DIM), axis=1)
    return cos_slab, sin_slab


def kernel(x, c, ctx, c_ctx, w_ada, b_ada, w_in, w_out, ln1_g, ln1_b, ln2_g, ln2_b,
           da_lq1, da_lk1, da_lq2, da_lk2, da_subln_g,
           lru_conv_w, lru_conv_b, lru_wr, lru_br, lru_wi, lru_bi, lru_lam,
           mla_qn_g, mla_wuq, mla_kvn_g, mla_wukv,
           ffn_wg, ffn_wu, ffn_wd):
    assert x.shape[0] == 1 and c.shape[0] == 1 and ctx.shape[0] == 1
    depth = w_ada.shape[0]
    n, d = x.shape[1], x.shape[2]
    nc = ctx.shape[1]
    alpha = (2 * depth) ** 0.25
    x_lat, x_ctx = x[0], ctx[0]

    cos_l, sin_l = _rope_tables(n)
    cos_c, sin_c = jnp.ones((nc, LANES), F32), jnp.zeros((nc, LANES), F32)

    c8 = jnp.zeros((SUBLANES, d), F32).at[0].set(c[0]).at[1].set(c_ctx)
    mods = _ada_mods(c8, w_ada, b_ada)

    row2 = lambda v: v.reshape(1, -1)
    for l in range(depth):
        need_ctx = l < depth - 1
        lambda_init = 0.8 - 0.6 * math.exp(-0.3 * l)
        m_lat = [mods[l, 0:1, i * d:(i + 1) * d] for i in range(6)]
        m_ctx = [mods[l, 1:2, i * d:(i + 1) * d] for i in range(6)]

        w_in_p = _pad_in_weight(w_in[l])
        w_out_b = w_out[l].astype(BF16)
        wq_p, wk_p, wv_p = _pad_mla_weights(mla_wuq[l], mla_wukv[l])
        wg_b, wu_b, wd_b = ffn_wg[l].astype(BF16), ffn_wu[l].astype(BF16), ffn_wd[l].astype(BF16)

        ql, kl, vl, lru_l, mla_l, fft_l = _inproj(x_lat, m_lat[0], m_lat[1], cos_l, sin_l, w_in_p, 512)
        qc, kc, vc, lru_c, mla_c, fft_c = _inproj(x_ctx, m_ctx[0], m_ctx[1], cos_c, sin_c, w_in_p, 256)

        lams = [row2(da_lq1[l]), row2(da_lk1[l]), row2(da_lq2[l]), row2(da_lk2[l])]
        g_da = row2(da_subln_g[l])
        da_l = _diff_attention(ql, kl, vl, (kc, vc), lams, g_da, lambda_init, ATTN_TQ, ATTN_TK, ATTN_CHUNK)

        lru_args = lambda dr: (lru_conv_w[l], row2(lru_conv_b[l]), lru_wr[l, dr].astype(BF16), row2(lru_br[l, dr]),
                               lru_wi[l, dr].astype(BF16), row2(lru_bi[l, dr]), row2(lru_lam[l, dr]))
        h0 = jnp.zeros((1, GROUP_WIDTH), F32)
        hc_f, s_f = _lru_pass(lru_c, *lru_args(0), h0, None, False, 256)
        lru_yc, s_b = _lru_pass(lru_c, *lru_args(1), h0, hc_f, True, 256)
        hl_f, _ = _lru_pass(lru_l, *lru_args(0), s_f, None, False, 256)
        lru_yl, _ = _lru_pass(lru_l, *lru_args(1), s_b, hl_f, True, 256)

        mq_l, mk_l, mv_l = _mla_prep(mla_l, cos_l, sin_l, row2(mla_qn_g[l]), row2(mla_kvn_g[l]), wq_p, wk_p, wv_p, 512)
        mq_c, mk_c, mv_c = _mla_prep(mla_c, cos_c, sin_c, row2(mla_qn_g[l]), row2(mla_kvn_g[l]), wq_p, wk_p, wv_p, 256)
        mla_yl = _mla_attention(mq_l, mk_l, mv_l, (mk_c, mv_c), ATTN_TQ, 2 * ATTN_TK, ATTN_CHUNK)

        fft_yl = _fourier_long(fft_l)

        x_lat = _outproj_ln([da_l, lru_yl, mla_yl, fft_yl], x_lat, m_lat[2], w_out_b,
                            row2(ln1_g[l]), row2(ln1_b[l]), alpha, 512)
        x_lat = _ffn_ln(x_lat, m_lat[3], m_lat[4], m_lat[5], wg_b, wu_b, wd_b,
                        row2(ln2_g[l]), row2(ln2_b[l]), alpha, 512, 512)

        if need_ctx:
            da_c = _diff_attention(qc, kc, vc, None, lams, g_da, lambda_init, 256, 256, 256)
            mla_yc = _mla_attention(mq_c, mk_c, mv_c, None, 256, 256, 256)
            fft_yc = _fourier_small(fft_c)
            x_ctx = _outproj_ln([da_c, lru_yc, mla_yc, fft_yc], x_ctx, m_ctx[2], w_out_b,
                                row2(ln1_g[l]), row2(ln1_b[l]), alpha, 256)
            x_ctx = _ffn_ln(x_ctx, m_ctx[3], m_ctx[4], m_ctx[5], wg_b, wu_b, wd_b,
                            row2(ln2_g[l]), row2(ln2_b[l]), alpha, 256, 512)
    return x_lat[None]
```

```python
import functools
import math

import numpy as np
import jax
import jax.numpy as jnp
from jax import lax
from jax.experimental import pallas as pl
from jax.experimental.pallas import tpu as pltpu

F32 = jnp.float32
BF16 = jnp.bfloat16

D_MODEL = 2048
GRID_W = 64
ROPE_BASE = 10000.0
LN_EPS = 1e-5
RMS_EPS = 1e-6

HEADS = 4
DA_HEAD_DIM = 64
GROUP_WIDTH = 512
LRU_BLOCKS = 4
LRU_BLOCK = GROUP_WIDTH // LRU_BLOCKS
CONV_W = 4
CONV_LEFT = 2
LRU_C = 8.0
MLA_Q_RANK = 384
MLA_KV_RANK = 256
MLA_NOPE = 128
MLA_ROPE = 64
MLA_V = 128
FFT_GROUPS = 4
FFT_GROUP = 128
ROT_DIM = 64
ROT_HALF = ROT_DIM // 2

LANES = 128
SUBLANES = 8
SEG_Q = 0
SEG_K = 512
SEG_V = 1024
SEG_LRU = 1536
SEG_MLA = 2560
SEG_FFT = 3328
IN_PAD = 3840
MLA_SEG = 768
MLA_HEAD_PAD = 256

FFT_N1 = 128

VMEM_LIMIT = 56 * 1024 * 1024
LOG2E = math.log2(math.e)

ATTN_TQ = 512
ATTN_TK = 4096
ATTN_CHUNK = 512
SOFTMAX_SLAB = 64
ONES_ROWS = 16
MAX_EXCESS = 32.0


def _cparams(*sem):
    return pltpu.CompilerParams(dimension_semantics=sem, vmem_limit_bytes=VMEM_LIMIT)


def _row_tile(n, want):
    t = min(n, want)
    assert n % t == 0, (n, t)
    return t


def _ada_kernel(c_ref, w_ref, b_ref, o_ref):
    s = c_ref[...]
    s = s * jax.nn.sigmoid(s)
    o_ref[...] = jnp.dot(s.astype(BF16), w_ref[...].astype(BF16), preferred_element_type=F32) + b_ref[...]


def _ada_mods(c8, w_ada, b_ada):
    depth, d, n6 = w_ada.shape
    tn = 1536
    return pl.pallas_call(
        _ada_kernel,
        grid=(depth, n6 // tn),
        in_specs=[
            pl.BlockSpec((SUBLANES, d), lambda l, j: (0, 0)),
            pl.BlockSpec((None, d, tn), lambda l, j: (l, 0, j)),
            pl.BlockSpec((None, 1, tn), lambda l, j: (l, 0, j)),
        ],
        out_specs=pl.BlockSpec((None, SUBLANES, tn), lambda l, j: (l, 0, j)),
        out_shape=jax.ShapeDtypeStruct((depth, SUBLANES, n6), F32),
        compiler_params=_cparams("arbitrary", "arbitrary"),
        name="ada_mods",
    )(c8, w_ada, b_ada.reshape(depth, 1, n6))


def _rope_slab(x, cos, sin_signed):
    lane = lax.broadcasted_iota(jnp.int32, x.shape, 1)
    partner = jnp.where((lane & ROT_HALF) == 0,
                        pltpu.roll(x, LANES - ROT_HALF, 1),
                        pltpu.roll(x, ROT_HALF, 1))
    return x * cos + partner * sin_signed


def _inproj_kernel(x_ref, sh_ref, sc_ref, cos_ref, sin_ref, w_ref,
                   q_ref, k_ref, v_ref, lru_ref, mla_ref, fft_ref, *, q_scale):
    xm = (x_ref[...] * (1.0 + sc_ref[...]) + sh_ref[...]).astype(BF16)
    cos = cos_ref[...]
    sin = sin_ref[...]

    def seg(a, width):
        return jnp.dot(xm, w_ref[:, a:a + width], preferred_element_type=F32)

    for h in range(HEADS):
        o = h * LANES
        q_ref[:, o:o + LANES] = (_rope_slab(seg(SEG_Q + o, LANES), cos, sin) * q_scale).astype(BF16)
        k_ref[:, o:o + LANES] = _rope_slab(seg(SEG_K + o, LANES), cos, sin).astype(BF16)
    v_ref[...] = seg(SEG_V, GROUP_WIDTH).T.astype(BF16)
    lru_ref[...] = seg(SEG_LRU, 2 * GROUP_WIDTH)
    mla_ref[...] = seg(SEG_MLA, MLA_SEG)
    fft_ref[...] = seg(SEG_FFT, GROUP_WIDTH).astype(BF16)


def _inproj(x, shift, scale, cos, sin, w_in_p, tm):
    n, d = x.shape
    tm = _row_tile(n, tm)
    row = lambda i: (i, 0)
    col = lambda i: (0, i)
    fixed = lambda i: (0, 0)
    outs = [
        jax.ShapeDtypeStruct((n, GROUP_WIDTH), BF16),
        jax.ShapeDtypeStruct((n, GROUP_WIDTH), BF16),
        jax.ShapeDtypeStruct((GROUP_WIDTH, n), BF16),
        jax.ShapeDtypeStruct((n, 2 * GROUP_WIDTH), F32),
        jax.ShapeDtypeStruct((n, MLA_SEG), F32),
        jax.ShapeDtypeStruct((n, GROUP_WIDTH), BF16),
    ]
    out_specs = [pl.BlockSpec((tm, o.shape[1]), row) for o in outs]
    out_specs[2] = pl.BlockSpec((GROUP_WIDTH, tm), col)
    return pl.pallas_call(
        functools.partial(_inproj_kernel, q_scale=DA_HEAD_DIM ** -0.5 * LOG2E),
        grid=(n // tm,),
        in_specs=[
            pl.BlockSpec((tm, d), row),
            pl.BlockSpec((1, d), fixed),
            pl.BlockSpec((1, d), fixed),
            pl.BlockSpec((tm, LANES), row),
            pl.BlockSpec((tm, LANES), row),
            pl.BlockSpec((d, IN_PAD), fixed),
        ],
        out_specs=out_specs,
        out_shape=outs,
        compiler_params=_cparams("arbitrary"),
        name="in_proj",
    )(x, shift, scale, cos, sin, w_in_p)


def _scores_t(q_t, k):
    return jnp.dot(k, q_t, preferred_element_type=F32)


def _probs_t(s_t, ref_max):
    keys = s_t.shape[0]
    slab = min(keys, SOFTMAX_SLAB)
    return jnp.concatenate([jnp.exp2(s_t[r:r + slab] - ref_max).astype(BF16) for r in range(0, keys, slab)], axis=0)


def _weighted_values_t(v_t, p_t):
    v_ext = jnp.concatenate([v_t, jnp.ones((ONES_ROWS, v_t.shape[1]), BF16)], axis=0)
    return jnp.dot(v_ext, p_t, preferred_element_type=F32)


def _softmax_update(s_t, v_t, m_ref, acc_ref, c):
    m_prev = m_ref[c]
    m_new = jnp.maximum(m_prev, jnp.max(s_t, axis=0, keepdims=True))
    alpha = jnp.exp2(m_prev - m_new)
    acc_ref[c] = alpha * acc_ref[c] + _weighted_values_t(v_t, _probs_t(s_t, m_new))
    m_ref[c] = m_new


def _flash_kernel(*refs, n_comp, has_prefix, chunk, epilogue):
    refs = list(refs)
    q_ref = refs.pop(0)
    if has_prefix:
        kc_ref, vtc_ref = refs.pop(0), refs.pop(0)
    k_ref, vt_ref = refs.pop(0), refs.pop(0)
    qt_ref, m_ref, acc_ref = refs[-3:]
    o_ref = refs[-4]
    extra = refs[:-4]
    j = pl.program_id(2)

    @pl.when(j == 0)
    def _():
        m_ref[...] = jnp.full(m_ref.shape, -jnp.inf, F32)
        acc_ref[...] = jnp.zeros(acc_ref.shape, F32)
        q_t = q_ref[...].astype(F32).T
        if n_comp == 2:
            row = lax.broadcasted_iota(jnp.int32, q_t.shape, 0)
            qt_ref[0] = jnp.where(row < DA_HEAD_DIM, q_t, 0.0).astype(BF16)
            qt_ref[1] = jnp.where(row >= DA_HEAD_DIM, q_t, 0.0).astype(BF16)
        else:
            qt_ref[0] = q_t.astype(BF16)
        if has_prefix:
            for c in range(n_comp):
                _softmax_update(_scores_t(qt_ref[c], kc_ref[...]), vtc_ref[...], m_ref, acc_ref, c)

    n_chunks = k_ref.shape[0] // chunk

    def exact_step():
        def body(i, carry):
            off = pl.multiple_of(i * chunk, chunk)
            for c in range(n_comp):
                _softmax_update(_scores_t(qt_ref[c], k_ref[pl.ds(off, chunk), :]), vt_ref[:, pl.ds(off, chunk)],
                                m_ref, acc_ref, c)
            return carry
        lax.fori_loop(0, n_chunks, body, 0)

    if not has_prefix:
        exact_step()
    else:
        ref_max = [m_ref[c] for c in range(n_comp)]
        step_acc = [None] * n_comp
        step_max = [None] * n_comp
        for i in range(n_chunks):
            off = i * chunk
            for c in range(n_comp):
                s_t = _scores_t(qt_ref[c], k_ref[off:off + chunk, :])
                unit_max = jnp.max(s_t, axis=0, keepdims=True)
                pv = _weighted_values_t(vt_ref[:, off:off + chunk], _probs_t(s_t, ref_max[c]))
                step_acc[c] = pv if step_acc[c] is None else step_acc[c] + pv
                step_max[c] = unit_max if step_max[c] is None else jnp.maximum(step_max[c], unit_max)
        excess = jnp.max(step_max[0] - ref_max[0])
        for c in range(1, n_comp):
            excess = jnp.maximum(excess, jnp.max(step_max[c] - ref_max[c]))
        in_range = excess <= MAX_EXCESS

        @pl.when(in_range)
        def _():
            for c in range(n_comp):
                m_new = jnp.maximum(ref_max[c], step_max[c])
                acc_ref[c] = (acc_ref[c] + step_acc[c]) * jnp.exp2(ref_max[c] - m_new)
                m_ref[c] = m_new

        @pl.when(jnp.logical_not(in_range))
        def _():
            exact_step()

    @pl.when(j == pl.num_programs(2) - 1)
    def _():
        epilogue(extra, o_ref, acc_ref)


def _normalised_t(acc_ref, c):
    dv = acc_ref.shape[1] - ONES_ROWS
    return acc_ref[c, 0:dv, :] / acc_ref[c, dv:dv + 1, :]


def _da_epilogue(extra, o_ref, acc_ref, *, lambda_init):
    lq1_ref, lk1_ref, lq2_ref, lk2_ref, g_ref = extra
    lam = (jnp.exp(jnp.sum(lq1_ref[...] * lk1_ref[...], keepdims=True))
           - jnp.exp(jnp.sum(lq2_ref[...] * lk2_ref[...], keepdims=True)) + lambda_init)
    o = (_normalised_t(acc_ref, 0) - lam * _normalised_t(acc_ref, 1)).T
    inv = lax.rsqrt(jnp.mean(o * o, axis=-1, keepdims=True) + RMS_EPS)
    o_ref[...] = ((o * inv * g_ref[...]) * (1.0 - lambda_init)).astype(o_ref.dtype)


def _mla_epilogue(extra, o_ref, acc_ref):
    o_ref[...] = _normalised_t(acc_ref, 0).T.astype(o_ref.dtype)


def _flash_attention(q, k, v_t, prefix, extra, extra_specs, *, dk, n_comp, epilogue, tq, tk, chunk, name):
    nq, nk = q.shape[0], k.shape[0]
    dv = v_t.shape[0] // HEADS
    tq, tk = _row_tile(nq, tq), _row_tile(nk, tk)
    chunk = _row_tile(tk, chunk)
    has_prefix = prefix is not None
    qspec = pl.BlockSpec((tq, dk), lambda h, i, j: (i, h))
    in_specs, args = [qspec], [q]
    if has_prefix:
        nc = prefix[0].shape[0]
        in_specs += [pl.BlockSpec((nc, dk), lambda h, i, j: (0, h)),
                     pl.BlockSpec((dv, nc), lambda h, i, j: (h, 0))]
        args += list(prefix)
    in_specs += [pl.BlockSpec((tk, dk), lambda h, i, j: (j, h)),
                 pl.BlockSpec((dv, tk), lambda h, i, j: (h, j))] + list(extra_specs)
    args += [k, v_t] + list(extra)
    return pl.pallas_call(
        functools.partial(_flash_kernel, n_comp=n_comp, has_prefix=has_prefix, chunk=chunk, epilogue=epilogue),
        grid=(HEADS, nq // tq, nk // tk),
        in_specs=in_specs,
        out_specs=pl.BlockSpec((tq, dv), lambda h, i, j: (i, h)),
        out_shape=jax.ShapeDtypeStruct((nq, HEADS * dv), BF16),
        scratch_shapes=[
            pltpu.VMEM((n_comp, dk, tq), BF16),
            pltpu.VMEM((n_comp, 1, tq), F32),
            pltpu.VMEM((n_comp, dv + ONES_ROWS, tq), F32),
        ],
        compiler_params=_cparams("arbitrary", "arbitrary", "arbitrary"),
        name=name,
    )(*args)


def _diff_attention(q, k, v_t, prefix, lams, g, lambda_init, tq, tk, chunk):
    small = lambda w: pl.BlockSpec((1, w), lambda h, i, j: (0, 0))
    return _flash_attention(
        q, k, v_t, prefix, list(lams) + [g], [small(DA_HEAD_DIM)] * 4 + [small(LANES)],
        dk=2 * DA_HEAD_DIM, n_comp=2, epilogue=functools.partial(_da_epilogue, lambda_init=lambda_init),
        tq=tq, tk=tk, chunk=chunk, name="diff_attention")


def _mla_attention(q, k, v_t, prefix, tq, tk, chunk):
    return _flash_attention(q, k, v_t, prefix, [], [], dk=MLA_HEAD_PAD, n_comp=1, epilogue=_mla_epilogue,
                            tq=tq, tk=tk, chunk=chunk, name="mla_attention")


def _rms(x, g):
    inv = lax.rsqrt(jnp.mean(x * x, axis=-1, keepdims=True) + RMS_EPS)
    return x * inv * g


def _mla_prep_kernel(u_ref, cos_ref, sin_ref, qg_ref, kvg_ref, wq_ref, wk_ref, wv_ref,
                     q_ref, k_ref, v_ref, *, scale):
    cos = cos_ref[...]
    sin = sin_ref[...]
    cq = _rms(u_ref[:, 0:MLA_Q_RANK], qg_ref[...]).astype(BF16)
    ckv = _rms(u_ref[:, MLA_Q_RANK:MLA_Q_RANK + MLA_KV_RANK], kvg_ref[...]).astype(BF16)
    k_rope = _rope_slab(u_ref[:, MLA_Q_RANK + MLA_KV_RANK:MLA_SEG], cos, sin).astype(BF16)
    for h in range(HEADS):
        o = h * MLA_HEAD_PAD
        q_nope = jnp.dot(cq, wq_ref[:, o:o + LANES], preferred_element_type=F32)
        q_rope = jnp.dot(cq, wq_ref[:, o + LANES:o + 2 * LANES], preferred_element_type=F32)
        q_ref[:, o:o + LANES] = (q_nope * scale).astype(BF16)
        q_ref[:, o + LANES:o + 2 * LANES] = (_rope_slab(q_rope, cos, sin) * scale).astype(BF16)
        k_nope = jnp.dot(ckv, wk_ref[:, h * LANES:(h + 1) * LANES], preferred_element_type=F32)
        k_ref[:, o:o + LANES] = k_nope.astype(BF16)
        k_ref[:, o + LANES:o + 2 * LANES] = k_rope
    v_ref[...] = jnp.dot(ckv, wv_ref[...], preferred_element_type=F32).T.astype(BF16)


def _mla_prep(u_mla, cos, sin, qn_g, kvn_g, wq_p, wk_p, wv_p, tm):
    n = u_mla.shape[0]
    tm = _row_tile(n, tm)
    row = lambda i: (i, 0)
    fixed = lambda i: (0, 0)
    outs = [
        jax.ShapeDtypeStruct((n, HEADS * MLA_HEAD_PAD), BF16),
        jax.ShapeDtypeStruct((n, HEADS * MLA_HEAD_PAD), BF16),
        jax.ShapeDtypeStruct((GROUP_WIDTH, n), BF16),
    ]
    out_specs = [pl.BlockSpec((tm, HEADS * MLA_HEAD_PAD), row)] * 2 + [pl.BlockSpec((GROUP_WIDTH, tm), lambda i: (0, i))]
    return pl.pallas_call(
        functools.partial(_mla_prep_kernel, scale=(MLA_NOPE + MLA_ROPE) ** -0.5 * LOG2E),
        grid=(n // tm,),
        in_specs=[
            pl.BlockSpec((tm, MLA_SEG), row),
            pl.BlockSpec((tm, LANES), row),
            pl.BlockSpec((tm, LANES), row),
            pl.BlockSpec((1, MLA_Q_RANK), fixed),
            pl.BlockSpec((1, MLA_KV_RANK), fixed),
            pl.BlockSpec(wq_p.shape, fixed),
            pl.BlockSpec(wk_p.shape, fixed),
            pl.BlockSpec(wv_p.shape, fixed),
        ],
        out_specs=out_specs,
        out_shape=outs,
        compiler_params=_cparams("arbitrary"),
        name="mla_prep",
    )(u_mla, cos, sin, qn_g, kvn_g, wq_p, wk_p, wv_p)


def _shift_rows(x, d, halo):
    tt = x.shape[0]
    row8 = lax.broadcasted_iota(jnp.int32, (SUBLANES, x.shape[1]), 0)
    if d > 0:
        r = pltpu.roll(x, d, 0)
        f = pltpu.roll(halo, d, 0)
        first = jnp.where(row8 < d, f, r[:SUBLANES])
        return jnp.concatenate([first, r[SUBLANES:]], axis=0) if tt > SUBLANES else first
    r = pltpu.roll(x, tt + d, 0)
    f = pltpu.roll(halo, SUBLANES + d, 0)
    last = jnp.where(row8 >= SUBLANES + d, f, r[tt - SUBLANES:])
    return jnp.concatenate([r[:tt - SUBLANES], last], axis=0) if tt > SUBLANES else last


def _lru_kernel(*refs, reverse, final):
    if final:
        (x_ref, prev_ref, next_ref, cw_ref, cb_ref, wr_ref, br_ref, wi_ref, bi_ref, lam_ref, h0_ref,
         gate_ref, hf_ref, y_ref, hlast_ref, carry_ref) = refs
    else:
        (x_ref, prev_ref, next_ref, cw_ref, cb_ref, wr_ref, br_ref, wi_ref, bi_ref, lam_ref, h0_ref,
         y_ref, hlast_ref, carry_ref) = refs
    i = pl.program_id(0)
    nt = pl.num_programs(0)
    t = (nt - 1 - i) if reverse else i
    x = x_ref[...]
    tt = x.shape[0]

    @pl.when(i == 0)
    def _():
        carry_ref[...] = h0_ref[...]

    prev = prev_ref[...] * (t > 0).astype(F32)
    nxt = next_ref[...] * (t < nt - 1).astype(F32)
    cw = cw_ref[...]
    xc = (cw[0:1] * _shift_rows(x, 2, prev) + cw[1:2] * _shift_rows(x, 1, prev)
          + cw[2:3] * x + cw[3:4] * _shift_rows(x, -1, nxt)) + cb_ref[...]

    xb = xc.astype(BF16)

    def gate(w_ref, b_ref):
        z = jnp.concatenate(
            [jnp.dot(xb[:, b * LRU_BLOCK:(b + 1) * LRU_BLOCK], w_ref[b], preferred_element_type=F32)
             for b in range(LRU_BLOCKS)], axis=1)
        return jax.nn.sigmoid(z + b_ref[...])

    r = gate(wr_ref, br_ref)
    ig = gate(wi_ref, bi_ref)
    log_a = -LRU_C * r * jax.nn.softplus(-lam_ref[...])
    a = jnp.exp(log_a)
    u = jnp.sqrt(1.0 - jnp.exp(2.0 * log_a)) * (ig * xc)

    row = lax.broadcasted_iota(jnp.int32, a.shape, 0)
    big_a, big_b = a, u
    d = 1
    while d < tt:
        if reverse:
            valid = row < tt - d
            a_s = pltpu.roll(big_a, tt - d, 0)
            b_s = pltpu.roll(big_b, tt - d, 0)
        else:
            valid = row >= d
            a_s = pltpu.roll(big_a, d, 0)
            b_s = pltpu.roll(big_b, d, 0)
        big_b = jnp.where(valid, big_a * b_s + big_b, big_b)
        big_a = jnp.where(valid, big_a * a_s, big_a)
        d *= 2
    h = big_a * carry_ref[...] + big_b
    edge = h[0:1] if reverse else h[tt - 1:tt]
    carry_ref[...] = edge
    hlast_ref[...] = edge
    if final:
        y_ref[...] = ((hf_ref[...] + h) * jax.nn.gelu(gate_ref[...])).astype(y_ref.dtype)
    else:
        y_ref[...] = h


def _lru_pass(u_lru, conv_w, conv_b, wr, br, wi, bi, lam, h0, h_fwd, reverse, tt):
    n = u_lru.shape[0]
    tt = _row_tile(n, tt)
    nt = n // tt
    per8 = tt // SUBLANES
    final = h_fwd is not None
    pos = (lambda i: nt - 1 - i) if reverse else (lambda i: i)
    tile = lambda i: (pos(i), 0)
    fixed = lambda i: (0, 0)
    fixed3 = lambda i: (0, 0, 0)
    in_specs = [
        pl.BlockSpec((tt, GROUP_WIDTH), tile),
        pl.BlockSpec((SUBLANES, GROUP_WIDTH), lambda i: (jnp.maximum(pos(i) * per8 - 1, 0), 0)),
        pl.BlockSpec((SUBLANES, GROUP_WIDTH), lambda i: (jnp.minimum((pos(i) + 1) * per8, n // SUBLANES - 1), 0)),
        pl.BlockSpec((CONV_W, GROUP_WIDTH), fixed),
        pl.BlockSpec((1, GROUP_WIDTH), fixed),
        pl.BlockSpec((LRU_BLOCKS, LRU_BLOCK, LRU_BLOCK), fixed3),
        pl.BlockSpec((1, GROUP_WIDTH), fixed),
        pl.BlockSpec((LRU_BLOCKS, LRU_BLOCK, LRU_BLOCK), fixed3),
        pl.BlockSpec((1, GROUP_WIDTH), fixed),
        pl.BlockSpec((1, GROUP_WIDTH), fixed),
        pl.BlockSpec((1, GROUP_WIDTH), fixed),
    ]
    args = [u_lru, u_lru, u_lru, conv_w, conv_b, wr, br, wi, bi, lam, h0]
    if final:
        in_specs += [pl.BlockSpec((tt, GROUP_WIDTH), lambda i: (pos(i), 1)),
                     pl.BlockSpec((tt, GROUP_WIDTH), tile)]
        args += [u_lru, h_fwd]
    return pl.pallas_call(
        functools.partial(_lru_kernel, reverse=reverse, final=final),
        grid=(nt,),
        in_specs=in_specs,
        out_specs=[pl.BlockSpec((tt, GROUP_WIDTH), tile), pl.BlockSpec((1, GROUP_WIDTH), fixed)],
        out_shape=[jax.ShapeDtypeStruct((n, GROUP_WIDTH), BF16 if final else F32),
                   jax.ShapeDtypeStruct((1, GROUP_WIDTH), F32)],
        scratch_shapes=[pltpu.VMEM((1, GROUP_WIDTH), F32)],
        compiler_params=_cparams("arbitrary"),
        name="rglru_bwd" if reverse else "rglru_fwd",
    )(*args)


def _dft_tables(n):
    k = np.arange(n, dtype=np.int64)
    ang = 2.0 * np.pi * ((k[:, None] * k[None, :]) % n).astype(np.float64) / n
    return np.cos(ang), np.sin(ang)


def _fft_small_kernel(g_ref, cs_ref, cn_ref, sn_ref, o_ref, *, norm):
    g = g_ref[...]
    a_parts, b_parts = [], []
    for grp in range(FFT_GROUPS):
        gg = g[:, grp * FFT_GROUP:(grp + 1) * FFT_GROUP]
        ab = jnp.dot(gg, cs_ref[...], preferred_element_type=F32)
        a_parts.append(ab[:, :FFT_GROUP])
        b_parts.append(ab[:, FFT_GROUP:])
    a = jnp.concatenate(a_parts, axis=1).astype(BF16)
    b = jnp.concatenate(b_parts, axis=1).astype(BF16)
    y = (jnp.dot(cn_ref[...], a, preferred_element_type=F32)
         - jnp.dot(sn_ref[...], b, preferred_element_type=F32))
    o_ref[...] = (y * norm).astype(o_ref.dtype)


def _fourier_small(g):
    n = g.shape[0]
    cc, sc = _dft_tables(FFT_GROUP)
    cn, sn = _dft_tables(n)
    cs = jnp.asarray(np.concatenate([cc, sc], axis=1), BF16)
    return pl.pallas_call(
        functools.partial(_fft_small_kernel, norm=float((n * FFT_GROUP) ** -0.5)),
        out_shape=jax.ShapeDtypeStruct((n, GROUP_WIDTH), BF16),
        compiler_params=pltpu.CompilerParams(vmem_limit_bytes=VMEM_LIMIT),
        name="fourier_ctx",
    )(g, cs, jnp.asarray(cn, BF16), jnp.asarray(sn, BF16))


def _fft_stage1_kernel(f_ref, x_ref, z_ref):
    res = jnp.dot(f_ref[...], x_ref[...], preferred_element_type=F32)
    tj = z_ref.shape[2]
    res = res.reshape(2, FFT_N1, tj * GROUP_WIDTH)
    for jj in range(tj):
        z_ref[:, :, jj, :] = res[:, :, jj * GROUP_WIDTH:(jj + 1) * GROUP_WIDTH]


def _fft_stage2_kernel(z_ref, twr_ref, twi_ref, f2_ref, cs_ref, o_ref, *, norm):
    tk1 = z_ref.shape[1]
    for i in range(tk1):
        zr = z_ref[0, i]
        zi = z_ref[1, i]
        twr = jnp.concatenate([twr_ref[i]] * (GROUP_WIDTH // LANES), axis=1)
        twi = jnp.concatenate([twi_ref[i]] * (GROUP_WIDTH // LANES), axis=1)
        zz = jnp.concatenate([zr * twr - zi * twi, zr * twi + zi * twr], axis=0).astype(BF16)
        p = jnp.dot(f2_ref[...], zz, preferred_element_type=F32)
        n2 = p.shape[0] // 2
        pr = p[:n2].astype(BF16)
        pim = p[n2:].astype(BF16)
        outs = []
        for grp in range(FFT_GROUPS):
            sl = slice(grp * FFT_GROUP, (grp + 1) * FFT_GROUP)
            lhs = jnp.concatenate([pr[:, sl], pim[:, sl]], axis=1)
            outs.append(jnp.dot(lhs, cs_ref[...], preferred_element_type=F32))
        o_ref[:, i, :] = (jnp.concatenate(outs, axis=1) * norm).astype(o_ref.dtype)


def _fourier_long(g):
    n = g.shape[0]
    n1 = FFT_N1
    assert n % (n1 * SUBLANES) == 0, n
    n2 = n // n1
    c1, s1 = _dft_tables(n1)
    f1 = jnp.asarray(np.concatenate([c1, -s1], axis=0), BF16)
    tj = min(n2, SUBLANES)
    z = pl.pallas_call(
        _fft_stage1_kernel,
        grid=(n2 // tj,),
        in_specs=[pl.BlockSpec((2 * n1, n1), lambda j: (0, 0)),
                  pl.BlockSpec((n1, tj * GROUP_WIDTH), lambda j: (0, j))],
        out_specs=pl.BlockSpec((2, n1, tj, GROUP_WIDTH), lambda j: (0, 0, j, 0)),
        out_shape=jax.ShapeDtypeStruct((2, n1, n2, GROUP_WIDTH), F32),
        compiler_params=_cparams("arbitrary"),
        name="fourier_stage1",
    )(f1, g.reshape(n1, n2 * GROUP_WIDTH))

    k1 = jnp.arange(n1, dtype=F32)[:, None]
    j2 = jnp.arange(n2, dtype=F32)[None, :]
    ang = (2.0 * np.pi / n) * (k1 * j2)
    twr = jnp.broadcast_to(jnp.cos(ang)[:, :, None], (n1, n2, LANES))
    twi = jnp.broadcast_to(-jnp.sin(ang)[:, :, None], (n1, n2, LANES))
    c2, s2 = _dft_tables(n2)
    f2 = jnp.asarray(np.block([[c2, s2], [-s2, c2]]), BF16)
    cc, sc = _dft_tables(FFT_GROUP)
    cs = jnp.asarray(np.concatenate([cc, sc], axis=0), BF16)
    tk1 = SUBLANES
    out = pl.pallas_call(
        functools.partial(_fft_stage2_kernel, norm=float((n * FFT_GROUP) ** -0.5)),
        grid=(n1 // tk1,),
        in_specs=[pl.BlockSpec((2, tk1, n2, GROUP_WIDTH), lambda i: (0, i, 0, 0)),
                  pl.BlockSpec((tk1, n2, LANES), lambda i: (i, 0, 0)),
                  pl.BlockSpec((tk1, n2, LANES), lambda i: (i, 0, 0)),
                  pl.BlockSpec((2 * n2, 2 * n2), lambda i: (0, 0)),
                  pl.BlockSpec((2 * FFT_GROUP, FFT_GROUP), lambda i: (0, 0))],
        out_specs=pl.BlockSpec((n2, tk1, GROUP_WIDTH), lambda i: (0, i, 0)),
        out_shape=jax.ShapeDtypeStruct((n2, n1, GROUP_WIDTH), BF16),
        compiler_params=_cparams("arbitrary"),
        name="fourier_stage2",
    )(z, twr, twi, f2, cs)
    return out.reshape(n, GROUP_WIDTH)


def _layernorm(z, g, b):
    mu = jnp.mean(z, axis=-1, keepdims=True)
    zc = z - mu
    var = jnp.mean(zc * zc, axis=-1, keepdims=True)
    return zc * lax.rsqrt(var + LN_EPS) * g + b


def _outproj_kernel(da_ref, lru_ref, mla_ref, fft_ref, x_ref, gate_ref, w_ref, g_ref, b_ref, o_ref, *, alpha):
    y = jnp.dot(da_ref[...], w_ref[0:GROUP_WIDTH], preferred_element_type=F32)
    y += jnp.dot(lru_ref[...], w_ref[GROUP_WIDTH:2 * GROUP_WIDTH], preferred_element_type=F32)
    y += jnp.dot(mla_ref[...], w_ref[2 * GROUP_WIDTH:3 * GROUP_WIDTH], preferred_element_type=F32)
    y += jnp.dot(fft_ref[...], w_ref[3 * GROUP_WIDTH:4 * GROUP_WIDTH], preferred_element_type=F32)
    z = alpha * x_ref[...] + gate_ref[...] * y
    o_ref[...] = _layernorm(z, g_ref[...], b_ref[...])


def _outproj_ln(parts, x, gate, w_out, ln_g, ln_b, alpha, tm):
    n, d = x.shape
    tm = _row_tile(n, tm)
    row = lambda i: (i, 0)
    fixed = lambda i: (0, 0)
    return pl.pallas_call(
        functools.partial(_outproj_kernel, alpha=alpha),
        grid=(n // tm,),
        in_specs=[pl.BlockSpec((tm, GROUP_WIDTH), row)] * 4 + [
            pl.BlockSpec((tm, d), row),
            pl.BlockSpec((1, d), fixed),
            pl.BlockSpec(w_out.shape, fixed),
            pl.BlockSpec((1, d), fixed),
            pl.BlockSpec((1, d), fixed),
        ],
        out_specs=pl.BlockSpec((tm, d), row),
        out_shape=jax.ShapeDtypeStruct((n, d), F32),
        compiler_params=_cparams("arbitrary"),
        name="out_proj_ln",
    )(*parts, x, gate, w_out, ln_g, ln_b)


def _ffn_kernel(x_ref, sh_ref, sc_ref, gate_ref, wg_ref, wu_ref, wd_ref, g_ref, b_ref, o_ref,
                h_ref, acc_ref, *, alpha):
    f = pl.program_id(1)

    @pl.when(f == 0)
    def _():
        h_ref[...] = (x_ref[...] * (1.0 + sc_ref[...]) + sh_ref[...]).astype(BF16)
        acc_ref[...] = jnp.zeros(acc_ref.shape, F32)

    h = h_ref[...]
    a = jnp.dot(h, wg_ref[...], preferred_element_type=F32)
    u = jnp.dot(h, wu_ref[...], preferred_element_type=F32)
    act = (a * jax.nn.sigmoid(a) * u).astype(BF16)
    acc_ref[...] += jnp.dot(act, wd_ref[...], preferred_element_type=F32)

    @pl.when(f == pl.num_programs(1) - 1)
    def _():
        z = alpha * x_ref[...] + gate_ref[...] * acc_ref[...]
        o_ref[...] = _layernorm(z, g_ref[...], b_ref[...])


def _ffn_ln(x, shift, scale, gate, wg, wu, wd, ln_g, ln_b, alpha, tm, tf):
    n, d = x.shape
    d_ff = wg.shape[1]
    tm = _row_tile(n, tm)
    assert d_ff % tf == 0
    row = lambda i, f: (i, 0)
    fixed = lambda i, f: (0, 0)
    return pl.pallas_call(
        functools.partial(_ffn_kernel, alpha=alpha),
        grid=(n // tm, d_ff // tf),
        in_specs=[
            pl.BlockSpec((tm, d), row),
            pl.BlockSpec((1, d), fixed),
            pl.BlockSpec((1, d), fixed),
            pl.BlockSpec((1, d), fixed),
            pl.BlockSpec((d, tf), lambda i, f: (0, f)),
            pl.BlockSpec((d, tf), lambda i, f: (0, f)),
            pl.BlockSpec((tf, d), lambda i, f: (f, 0)),
            pl.BlockSpec((1, d), fixed),
            pl.BlockSpec((1, d), fixed),
        ],
        out_specs=pl.BlockSpec((tm, d), row),
        out_shape=jax.ShapeDtypeStruct((n, d), F32),
        scratch_shapes=[pltpu.VMEM((tm, d), BF16), pltpu.VMEM((tm, d), F32)],
        compiler_params=_cparams("arbitrary", "arbitrary"),
        name="ffn_ln",
    )(x, shift, scale, gate, wg, wu, wd, ln_g, ln_b)


def _pad_in_weight(w_in):
    d = w_in.shape[0]
    src_fft = 3 * GROUP_WIDTH + 2 * GROUP_WIDTH + MLA_Q_RANK + MLA_KV_RANK + MLA_ROPE
    pad = jnp.zeros((d, SEG_FFT - (SEG_MLA + MLA_Q_RANK + MLA_KV_RANK + MLA_ROPE)), w_in.dtype)
    return jnp.concatenate([w_in[:, :src_fft], pad, w_in[:, src_fft:]], axis=1).astype(BF16)


def _pad_mla_weights(w_uq, w_ukv):
    qr = w_uq.shape[0]
    wq = w_uq.reshape(qr, HEADS, MLA_NOPE + MLA_ROPE)
    wq = jnp.concatenate([wq, jnp.zeros((qr, HEADS, MLA_HEAD_PAD - MLA_NOPE - MLA_ROPE), w_uq.dtype)], axis=2)
    wkv = w_ukv.reshape(w_ukv.shape[0], HEADS, MLA_NOPE + MLA_V)
    wk = wkv[:, :, :MLA_NOPE].reshape(w_ukv.shape[0], HEADS * MLA_NOPE)
    wv = wkv[:, :, MLA_NOPE:].reshape(w_ukv.shape[0], HEADS * MLA_V)
    return (wq.reshape(qr, HEADS * MLA_HEAD_PAD).astype(BF16), wk.astype(BF16), wv.astype(BF16))


def _rope_tables(n):
    rows = n // GRID_W
    row = jnp.repeat(jnp.arange(rows, dtype=F32), GRID_W)
    col = jnp.tile(jnp.arange(GRID_W, dtype=F32), rows)
    axis_dim = ROT_DIM // 2
    inv = ROPE_BASE ** (-jnp.arange(0, axis_dim, 2, dtype=F32) / axis_dim)
    ang = jnp.concatenate([row[:, None] * inv, col[:, None] * inv], axis=-1)
    cos, sin = jnp.cos(ang), jnp.sin(ang)
    cos_slab = jnp.concatenate([cos, cos] * (LANES // ROT_DIM), axis=1)
    sin_slab = jnp.concatenate([-sin, sin] * (LANES // ROT_DIM), axis=1)
    return cos_slab, sin_slab


def kernel(x, c, ctx, c_ctx, w_ada, b_ada, w_in, w_out, ln1_g, ln1_b, ln2_g, ln2_b,
           da_lq1, da_lk1, da_lq2, da_lk2, da_subln_g,
           lru_conv_w, lru_conv_b, lru_wr, lru_br, lru_wi, lru_bi, lru_lam,
           mla_qn_g, mla_wuq, mla_kvn_g, mla_wukv,
           ffn_wg, ffn_wu, ffn_wd):
    assert x.shape[0] == 1 and c.shape[0] == 1 and ctx.shape[0] == 1
    depth = w_ada.shape[0]
    n, d = x.shape[1], x.shape[2]
    nc = ctx.shape[1]
    alpha = (2 * depth) ** 0.25
    x_lat, x_ctx = x[0], ctx[0]

    cos_l, sin_l = _rope_tables(n)
    cos_c, sin_c = jnp.ones((nc, LANES), F32), jnp.zeros((nc, LANES), F32)

    c8 = jnp.zeros((SUBLANES, d), F32).at[0].set(c[0]).at[1].set(c_ctx)
    mods = _ada_mods(c8, w_ada, b_ada)

    row2 = lambda v: v.reshape(1, -1)
    for l in range(depth):
        need_ctx = l < depth - 1
        lambda_init = 0.8 - 0.6 * math.exp(-0.3 * l)
        m_lat = [mods[l, 0:1, i * d:(i + 1) * d] for i in range(6)]
        m_ctx = [mods[l, 1:2, i * d:(i + 1) * d] for i in range(6)]

        w_in_p = _pad_in_weight(w_in[l])
        w_out_b = w_out[l].astype(BF16)
        wq_p, wk_p, wv_p = _pad_mla_weights(mla_wuq[l], mla_wukv[l])
        wg_b, wu_b, wd_b = ffn_wg[l].astype(BF16), ffn_wu[l].astype(BF16), ffn_wd[l].astype(BF16)

        ql, kl, vl, lru_l, mla_l, fft_l = _inproj(x_lat, m_lat[0], m_lat[1], cos_l, sin_l, w_in_p, 512)
        qc, kc, vc, lru_c, mla_c, fft_c = _inproj(x_ctx, m_ctx[0], m_ctx[1], cos_c, sin_c, w_in_p, 256)

        lams = [row2(da_lq1[l]), row2(da_lk1[l]), row2(da_lq2[l]), row2(da_lk2[l])]
        g_da = row2(da_subln_g[l])
        da_l = _diff_attention(ql, kl, vl, (kc, vc), lams, g_da, lambda_init, ATTN_TQ, ATTN_TK, ATTN_CHUNK)

        lru_args = lambda dr: (lru_conv_w[l], row2(lru_conv_b[l]), lru_wr[l, dr].astype(BF16), row2(lru_br[l, dr]),
                               lru_wi[l, dr].astype(BF16), row2(lru_bi[l, dr]), row2(lru_lam[l, dr]))
        h0 = jnp.zeros((1, GROUP_WIDTH), F32)
        hc_f, s_f = _lru_pass(lru_c, *lru_args(0), h0, None, False, 256)
        lru_yc, s_b = _lru_pass(lru_c, *lru_args(1), h0, hc_f, True, 256)
        hl_f, _ = _lru_pass(lru_l, *lru_args(0), s_f, None, False, 256)
        lru_yl, _ = _lru_pass(lru_l, *lru_args(1), s_b, hl_f, True, 256)

        mq_l, mk_l, mv_l = _mla_prep(mla_l, cos_l, sin_l, row2(mla_qn_g[l]), row2(mla_kvn_g[l]), wq_p, wk_p, wv_p, 512)
        mq_c, mk_c, mv_c = _mla_prep(mla_c, cos_c, sin_c, row2(mla_qn_g[l]), row2(mla_kvn_g[l]), wq_p, wk_p, wv_p, 256)
        mla_yl = _mla_attention(mq_l, mk_l, mv_l, (mk_c, mv_c), ATTN_TQ, ATTN_TK, ATTN_CHUNK)

        fft_yl = _fourier_long(fft_l)

        x_lat = _outproj_ln([da_l, lru_yl, mla_yl, fft_yl], x_lat, m_lat[2], w_out_b,
                            row2(ln1_g[l]), row2(ln1_b[l]), alpha, 512)
        x_lat = _ffn_ln(x_lat, m_lat[3], m_lat[4], m_lat[5], wg_b, wu_b, wd_b,
                        row2(ln2_g[l]), row2(ln2_b[l]), alpha, 512, 512)

        if need_ctx:
            da_c = _diff_attention(qc, kc, vc, None, lams, g_da, lambda_init, 256, 256, 256)
            mla_yc = _mla_attention(mq_c, mk_c, mv_c, None, 256, 256, 256)
            fft_yc = _fourier_small(fft_c)
            x_ctx = _outproj_ln([da_c, lru_yc, mla_yc, fft_yc], x_ctx, m_ctx[2], w_out_b,
                                row2(ln1_g[l]), row2(ln1_b[l]), alpha, 256)
            x_ctx = _ffn_ln(x_ctx, m_ctx[3], m_ctx[4], m_ctx[5], wg_b, wu_b, wd_b,
                            row2(ln2_g[l]), row2(ln2_b[l]), alpha, 256, 512)
    return x_lat[None]
```

```python
import functools
import math

import numpy as np
import jax
import jax.numpy as jnp
from jax import lax
from jax.experimental import pallas as pl
from jax.experimental.pallas import tpu as pltpu

F32 = jnp.float32
BF16 = jnp.bfloat16

D_MODEL = 2048
GRID_W = 64
ROPE_BASE = 10000.0
LN_EPS = 1e-5
RMS_EPS = 1e-6

HEADS = 4
DA_HEAD_DIM = 64
GROUP_WIDTH = 512
LRU_BLOCKS = 4
LRU_BLOCK = GROUP_WIDTH // LRU_BLOCKS
CONV_W = 4
CONV_LEFT = 2
LRU_C = 8.0
MLA_Q_RANK = 384
MLA_KV_RANK = 256
MLA_NOPE = 128
MLA_ROPE = 64
MLA_V = 128
FFT_GROUPS = 4
FFT_GROUP = 128
ROT_DIM = 64
ROT_HALF = ROT_DIM // 2

LANES = 128
SUBLANES = 8
SEG_Q = 0
SEG_K = 512
SEG_V = 1024
SEG_LRU = 1536
SEG_MLA = 2560
SEG_FFT = 3328
IN_PAD = 3840
MLA_SEG = 768
MLA_HEAD_PAD = 256

FFT_N1 = 128

VMEM_LIMIT = 56 * 1024 * 1024
LOG2E = math.log2(math.e)

ATTN_TQ = 1024
ATTN_TK = 4096
ATTN_CHUNK = 512
SOFTMAX_SLAB = 64
ONES_ROWS = 16
MAX_EXCESS = 32.0


def _cparams(*sem):
    return pltpu.CompilerParams(dimension_semantics=sem, vmem_limit_bytes=VMEM_LIMIT)


def _row_tile(n, want):
    t = min(n, want)
    assert n % t == 0, (n, t)
    return t


def _ada_kernel(c_ref, w_ref, b_ref, o_ref):
    s = c_ref[...]
    s = s * jax.nn.sigmoid(s)
    o_ref[...] = jnp.dot(s.astype(BF16), w_ref[...].astype(BF16), preferred_element_type=F32) + b_ref[...]


def _ada_mods(c8, w_ada, b_ada):
    depth, d, n6 = w_ada.shape
    tn = 1536
    return pl.pallas_call(
        _ada_kernel,
        grid=(depth, n6 // tn),
        in_specs=[
            pl.BlockSpec((SUBLANES, d), lambda l, j: (0, 0)),
            pl.BlockSpec((None, d, tn), lambda l, j: (l, 0, j)),
            pl.BlockSpec((None, 1, tn), lambda l, j: (l, 0, j)),
        ],
        out_specs=pl.BlockSpec((None, SUBLANES, tn), lambda l, j: (l, 0, j)),
        out_shape=jax.ShapeDtypeStruct((depth, SUBLANES, n6), F32),
        compiler_params=_cparams("arbitrary", "arbitrary"),
        name="ada_mods",
    )(c8, w_ada, b_ada.reshape(depth, 1, n6))


def _rope_slab(x, cos, sin_signed):
    lane = lax.broadcasted_iota(jnp.int32, x.shape, 1)
    partner = jnp.where((lane & ROT_HALF) == 0,
                        pltpu.roll(x, LANES - ROT_HALF, 1),
                        pltpu.roll(x, ROT_HALF, 1))
    return x * cos + partner * sin_signed


def _inproj_kernel(x_ref, sh_ref, sc_ref, cos_ref, sin_ref, w_ref,
                   q_ref, k_ref, v_ref, lru_ref, mla_ref, fft_ref, *, q_scale):
    xm = (x_ref[...] * (1.0 + sc_ref[...]) + sh_ref[...]).astype(BF16)
    cos = cos_ref[...]
    sin = sin_ref[...]

    def seg(a, width):
        return jnp.dot(xm, w_ref[:, a:a + width], preferred_element_type=F32)

    for h in range(HEADS):
        o = h * LANES
        q_ref[:, o:o + LANES] = (_rope_slab(seg(SEG_Q + o, LANES), cos, sin) * q_scale).astype(BF16)
        k_ref[:, o:o + LANES] = _rope_slab(seg(SEG_K + o, LANES), cos, sin).astype(BF16)
    v_ref[...] = seg(SEG_V, GROUP_WIDTH).T.astype(BF16)
    lru_ref[...] = seg(SEG_LRU, 2 * GROUP_WIDTH)
    mla_ref[...] = seg(SEG_MLA, MLA_SEG)
    fft_ref[...] = seg(SEG_FFT, GROUP_WIDTH).astype(BF16)


def _inproj(x, shift, scale, cos, sin, w_in_p, tm):
    n, d = x.shape
    tm = _row_tile(n, tm)
    row = lambda i: (i, 0)
    col = lambda i: (0, i)
    fixed = lambda i: (0, 0)
    outs = [
        jax.ShapeDtypeStruct((n, GROUP_WIDTH), BF16),
        jax.ShapeDtypeStruct((n, GROUP_WIDTH), BF16),
        jax.ShapeDtypeStruct((GROUP_WIDTH, n), BF16),
        jax.ShapeDtypeStruct((n, 2 * GROUP_WIDTH), F32),
        jax.ShapeDtypeStruct((n, MLA_SEG), F32),
        jax.ShapeDtypeStruct((n, GROUP_WIDTH), BF16),
    ]
    out_specs = [pl.BlockSpec((tm, o.shape[1]), row) for o in outs]
    out_specs[2] = pl.BlockSpec((GROUP_WIDTH, tm), col)
    return pl.pallas_call(
        functools.partial(_inproj_kernel, q_scale=DA_HEAD_DIM ** -0.5 * LOG2E),
        grid=(n // tm,),
        in_specs=[
            pl.BlockSpec((tm, d), row),
            pl.BlockSpec((1, d), fixed),
            pl.BlockSpec((1, d), fixed),
            pl.BlockSpec((tm, LANES), row),
            pl.BlockSpec((tm, LANES), row),
            pl.BlockSpec((d, IN_PAD), fixed),
        ],
        out_specs=out_specs,
        out_shape=outs,
        compiler_params=_cparams("arbitrary"),
        name="in_proj",
    )(x, shift, scale, cos, sin, w_in_p)


def _scores_t(q_t, k):
    return jnp.dot(k, q_t, preferred_element_type=F32)


def _probs_t(s_t, ref_max):
    keys = s_t.shape[0]
    slab = min(keys, SOFTMAX_SLAB)
    return jnp.concatenate([jnp.exp2(s_t[r:r + slab] - ref_max).astype(BF16) for r in range(0, keys, slab)], axis=0)


def _weighted_values_t(v_t, p_t):
    v_ext = jnp.concatenate([v_t, jnp.ones((ONES_ROWS, v_t.shape[1]), BF16)], axis=0)
    return jnp.dot(v_ext, p_t, preferred_element_type=F32)


def _softmax_update(s_t, v_t, m_ref, acc_ref, c):
    m_prev = m_ref[c]
    m_new = jnp.maximum(m_prev, jnp.max(s_t, axis=0, keepdims=True))
    alpha = jnp.exp2(m_prev - m_new)
    acc_ref[c] = alpha * acc_ref[c] + _weighted_values_t(v_t, _probs_t(s_t, m_new))
    m_ref[c] = m_new


def _flash_kernel(*refs, n_comp, has_prefix, chunk, epilogue):
    refs = list(refs)
    q_ref = refs.pop(0)
    if has_prefix:
        kc_ref, vtc_ref = refs.pop(0), refs.pop(0)
    k_ref, vt_ref = refs.pop(0), refs.pop(0)
    qt_ref, m_ref, acc_ref = refs[-3:]
    o_ref = refs[-4]
    extra = refs[:-4]
    j = pl.program_id(2)

    @pl.when(j == 0)
    def _():
        m_ref[...] = jnp.full(m_ref.shape, -jnp.inf, F32)
        acc_ref[...] = jnp.zeros(acc_ref.shape, F32)
        q_t = q_ref[...].astype(F32).T
        if n_comp == 2:
            row = lax.broadcasted_iota(jnp.int32, q_t.shape, 0)
            qt_ref[0] = jnp.where(row < DA_HEAD_DIM, q_t, 0.0).astype(BF16)
            qt_ref[1] = jnp.where(row >= DA_HEAD_DIM, q_t, 0.0).astype(BF16)
        else:
            qt_ref[0] = q_t.astype(BF16)
        if has_prefix:
            for c in range(n_comp):
                _softmax_update(_scores_t(qt_ref[c], kc_ref[...]), vtc_ref[...], m_ref, acc_ref, c)

    n_chunks = k_ref.shape[0] // chunk

    def exact_step():
        def body(i, carry):
            off = pl.multiple_of(i * chunk, chunk)
            for c in range(n_comp):
                _softmax_update(_scores_t(qt_ref[c], k_ref[pl.ds(off, chunk), :]), vt_ref[:, pl.ds(off, chunk)],
                                m_ref, acc_ref, c)
            return carry
        lax.fori_loop(0, n_chunks, body, 0)

    if not has_prefix:
        exact_step()
    else:
        ref_max = [m_ref[c] for c in range(n_comp)]
        step_acc = [None] * n_comp
        step_max = [None] * n_comp
        for i in range(n_chunks):
            off = i * chunk
            for c in range(n_comp):
                s_t = _scores_t(qt_ref[c], k_ref[off:off + chunk, :])
                unit_max = jnp.max(s_t, axis=0, keepdims=True)
                pv = _weighted_values_t(vt_ref[:, off:off + chunk], _probs_t(s_t, ref_max[c]))
                step_acc[c] = pv if step_acc[c] is None else step_acc[c] + pv
                step_max[c] = unit_max if step_max[c] is None else jnp.maximum(step_max[c], unit_max)
        excess = jnp.max(step_max[0] - ref_max[0])
        for c in range(1, n_comp):
            excess = jnp.maximum(excess, jnp.max(step_max[c] - ref_max[c]))
        in_range = excess <= MAX_EXCESS

        @pl.when(in_range)
        def _():
            for c in range(n_comp):
                m_new = jnp.maximum(ref_max[c], step_max[c])
                acc_ref[c] = (acc_ref[c] + step_acc[c]) * jnp.exp2(ref_max[c] - m_new)
                m_ref[c] = m_new

        @pl.when(jnp.logical_not(in_range))
        def _():
            exact_step()

    @pl.when(j == pl.num_programs(2) - 1)
    def _():
        epilogue(extra, o_ref, acc_ref)


def _normalised_t(acc_ref, c):
    dv = acc_ref.shape[1] - ONES_ROWS
    return acc_ref[c, 0:dv, :] / acc_ref[c, dv:dv + 1, :]


def _da_epilogue(extra, o_ref, acc_ref, *, lambda_init):
    lq1_ref, lk1_ref, lq2_ref, lk2_ref, g_ref = extra
    lam = (jnp.exp(jnp.sum(lq1_ref[...] * lk1_ref[...], keepdims=True))
           - jnp.exp(jnp.sum(lq2_ref[...] * lk2_ref[...], keepdims=True)) + lambda_init)
    o = (_normalised_t(acc_ref, 0) - lam * _normalised_t(acc_ref, 1)).T
    inv = lax.rsqrt(jnp.mean(o * o, axis=-1, keepdims=True) + RMS_EPS)
    o_ref[...] = ((o * inv * g_ref[...]) * (1.0 - lambda_init)).astype(o_ref.dtype)


def _mla_epilogue(extra, o_ref, acc_ref):
    o_ref[...] = _normalised_t(acc_ref, 0).T.astype(o_ref.dtype)


def _flash_attention(q, k, v_t, prefix, extra, extra_specs, *, dk, n_comp, epilogue, tq, tk, chunk, name):
    nq, nk = q.shape[0], k.shape[0]
    dv = v_t.shape[0] // HEADS
    tq, tk = _row_tile(nq, tq), _row_tile(nk, tk)
    chunk = _row_tile(tk, chunk)
    has_prefix = prefix is not None
    qspec = pl.BlockSpec((tq, dk), lambda h, i, j: (i, h))
    in_specs, args = [qspec], [q]
    if has_prefix:
        nc = prefix[0].shape[0]
        in_specs += [pl.BlockSpec((nc, dk), lambda h, i, j: (0, h)),
                     pl.BlockSpec((dv, nc), lambda h, i, j: (h, 0))]
        args += list(prefix)
    in_specs += [pl.BlockSpec((tk, dk), lambda h, i, j: (j, h)),
                 pl.BlockSpec((dv, tk), lambda h, i, j: (h, j))] + list(extra_specs)
    args += [k, v_t] + list(extra)
    return pl.pallas_call(
        functools.partial(_flash_kernel, n_comp=n_comp, has_prefix=has_prefix, chunk=chunk, epilogue=epilogue),
        grid=(HEADS, nq // tq, nk // tk),
        in_specs=in_specs,
        out_specs=pl.BlockSpec((tq, dv), lambda h, i, j: (i, h)),
        out_shape=jax.ShapeDtypeStruct((nq, HEADS * dv), BF16),
        scratch_shapes=[
            pltpu.VMEM((n_comp, dk, tq), BF16),
            pltpu.VMEM((n_comp, 1, tq), F32),
            pltpu.VMEM((n_comp, dv + ONES_ROWS, tq), F32),
        ],
        compiler_params=_cparams("arbitrary", "arbitrary", "arbitrary"),
        name=name,
    )(*args)


def _diff_attention(q, k, v_t, prefix, lams, g, lambda_init, tq, tk, chunk):
    small = lambda w: pl.BlockSpec((1, w), lambda h, i, j: (0, 0))
    return _flash_attention(
        q, k, v_t, prefix, list(lams) + [g], [small(DA_HEAD_DIM)] * 4 + [small(LANES)],
        dk=2 * DA_HEAD_DIM, n_comp=2, epilogue=functools.partial(_da_epilogue, lambda_init=lambda_init),
        tq=tq, tk=tk, chunk=chunk, name="diff_attention")


def _mla_attention(q, k, v_t, prefix, tq, tk, chunk):
    return _flash_attention(q, k, v_t, prefix, [], [], dk=MLA_HEAD_PAD, n_comp=1, epilogue=_mla_epilogue,
                            tq=tq, tk=tk, chunk=chunk, name="mla_attention")


def _rms(x, g):
    inv = lax.rsqrt(jnp.mean(x * x, axis=-1, keepdims=True) + RMS_EPS)
    return x * inv * g


def _mla_prep_kernel(u_ref, cos_ref, sin_ref, qg_ref, kvg_ref, wq_ref, wk_ref, wv_ref,
                     q_ref, k_ref, v_ref, *, scale):
    cos = cos_ref[...]
    sin = sin_ref[...]
    cq = _rms(u_ref[:, 0:MLA_Q_RANK], qg_ref[...]).astype(BF16)
    ckv = _rms(u_ref[:, MLA_Q_RANK:MLA_Q_RANK + MLA_KV_RANK], kvg_ref[...]).astype(BF16)
    k_rope = _rope_slab(u_ref[:, MLA_Q_RANK + MLA_KV_RANK:MLA_SEG], cos, sin).astype(BF16)
    for h in range(HEADS):
        o = h * MLA_HEAD_PAD
        q_nope = jnp.dot(cq, wq_ref[:, o:o + LANES], preferred_element_type=F32)
        q_rope = jnp.dot(cq, wq_ref[:, o + LANES:o + 2 * LANES], preferred_element_type=F32)
        q_ref[:, o:o + LANES] = (q_nope * scale).astype(BF16)
        q_ref[:, o + LANES:o + 2 * LANES] = (_rope_slab(q_rope, cos, sin) * scale).astype(BF16)
        k_nope = jnp.dot(ckv, wk_ref[:, h * LANES:(h + 1) * LANES], preferred_element_type=F32)
        k_ref[:, o:o + LANES] = k_nope.astype(BF16)
        k_ref[:, o + LANES:o + 2 * LANES] = k_rope
    v_ref[...] = jnp.dot(ckv, wv_ref[...], preferred_element_type=F32).T.astype(BF16)


def _mla_prep(u_mla, cos, sin, qn_g, kvn_g, wq_p, wk_p, wv_p, tm):
    n = u_mla.shape[0]
    tm = _row_tile(n, tm)
    row = lambda i: (i, 0)
    fixed = lambda i: (0, 0)
    outs = [
        jax.ShapeDtypeStruct((n, HEADS * MLA_HEAD_PAD), BF16),
        jax.ShapeDtypeStruct((n, HEADS * MLA_HEAD_PAD), BF16),
        jax.ShapeDtypeStruct((GROUP_WIDTH, n), BF16),
    ]
    out_specs = [pl.BlockSpec((tm, HEADS * MLA_HEAD_PAD), row)] * 2 + [pl.BlockSpec((GROUP_WIDTH, tm), lambda i: (0, i))]
    return pl.pallas_call(
        functools.partial(_mla_prep_kernel, scale=(MLA_NOPE + MLA_ROPE) ** -0.5 * LOG2E),
        grid=(n // tm,),
        in_specs=[
            pl.BlockSpec((tm, MLA_SEG), row),
            pl.BlockSpec((tm, LANES), row),
            pl.BlockSpec((tm, LANES), row),
            pl.BlockSpec((1, MLA_Q_RANK), fixed),
            pl.BlockSpec((1, MLA_KV_RANK), fixed),
            pl.BlockSpec(wq_p.shape, fixed),
            pl.BlockSpec(wk_p.shape, fixed),
            pl.BlockSpec(wv_p.shape, fixed),
        ],
        out_specs=out_specs,
        out_shape=outs,
        compiler_params=_cparams("arbitrary"),
        name="mla_prep",
    )(u_mla, cos, sin, qn_g, kvn_g, wq_p, wk_p, wv_p)


def _shift_rows(x, d, halo):
    tt = x.shape[0]
    row8 = lax.broadcasted_iota(jnp.int32, (SUBLANES, x.shape[1]), 0)
    if d > 0:
        r = pltpu.roll(x, d, 0)
        f = pltpu.roll(halo, d, 0)
        first = jnp.where(row8 < d, f, r[:SUBLANES])
        return jnp.concatenate([first, r[SUBLANES:]], axis=0) if tt > SUBLANES else first
    r = pltpu.roll(x, tt + d, 0)
    f = pltpu.roll(halo, SUBLANES + d, 0)
    last = jnp.where(row8 >= SUBLANES + d, f, r[tt - SUBLANES:])
    return jnp.concatenate([r[:tt - SUBLANES], last], axis=0) if tt > SUBLANES else last


def _lru_kernel(*refs, reverse, final):
    if final:
        (x_ref, prev_ref, next_ref, cw_ref, cb_ref, wr_ref, br_ref, wi_ref, bi_ref, lam_ref, h0_ref,
         gate_ref, hf_ref, y_ref, hlast_ref, carry_ref) = refs
    else:
        (x_ref, prev_ref, next_ref, cw_ref, cb_ref, wr_ref, br_ref, wi_ref, bi_ref, lam_ref, h0_ref,
         y_ref, hlast_ref, carry_ref) = refs
    i = pl.program_id(0)
    nt = pl.num_programs(0)
    t = (nt - 1 - i) if reverse else i
    x = x_ref[...]
    tt = x.shape[0]

    @pl.when(i == 0)
    def _():
        carry_ref[...] = h0_ref[...]

    prev = prev_ref[...] * (t > 0).astype(F32)
    nxt = next_ref[...] * (t < nt - 1).astype(F32)
    cw = cw_ref[...]
    xc = (cw[0:1] * _shift_rows(x, 2, prev) + cw[1:2] * _shift_rows(x, 1, prev)
          + cw[2:3] * x + cw[3:4] * _shift_rows(x, -1, nxt)) + cb_ref[...]

    xb = xc.astype(BF16)

    def gate(w_ref, b_ref):
        z = jnp.concatenate(
            [jnp.dot(xb[:, b * LRU_BLOCK:(b + 1) * LRU_BLOCK], w_ref[b], preferred_element_type=F32)
             for b in range(LRU_BLOCKS)], axis=1)
        return jax.nn.sigmoid(z + b_ref[...])

    r = gate(wr_ref, br_ref)
    ig = gate(wi_ref, bi_ref)
    log_a = -LRU_C * r * jax.nn.softplus(-lam_ref[...])
    a = jnp.exp(log_a)
    u = jnp.sqrt(1.0 - jnp.exp(2.0 * log_a)) * (ig * xc)

    row = lax.broadcasted_iota(jnp.int32, a.shape, 0)
    big_a, big_b = a, u
    d = 1
    while d < tt:
        if reverse:
            valid = row < tt - d
            a_s = pltpu.roll(big_a, tt - d, 0)
            b_s = pltpu.roll(big_b, tt - d, 0)
        else:
            valid = row >= d
            a_s = pltpu.roll(big_a, d, 0)
            b_s = pltpu.roll(big_b, d, 0)
        big_b = jnp.where(valid, big_a * b_s + big_b, big_b)
        big_a = jnp.where(valid, big_a * a_s, big_a)
        d *= 2
    h = big_a * carry_ref[...] + big_b
    edge = h[0:1] if reverse else h[tt - 1:tt]
    carry_ref[...] = edge
    hlast_ref[...] = edge
    if final:
        y_ref[...] = ((hf_ref[...] + h) * jax.nn.gelu(gate_ref[...])).astype(y_ref.dtype)
    else:
        y_ref[...] = h


def _lru_pass(u_lru, conv_w, conv_b, wr, br, wi, bi, lam, h0, h_fwd, reverse, tt):
    n = u_lru.shape[0]
    tt = _row_tile(n, tt)
    nt = n // tt
    per8 = tt // SUBLANES
    final = h_fwd is not None
    pos = (lambda i: nt - 1 - i) if reverse else (lambda i: i)
    tile = lambda i: (pos(i), 0)
    fixed = lambda i: (0, 0)
    fixed3 = lambda i: (0, 0, 0)
    in_specs = [
        pl.BlockSpec((tt, GROUP_WIDTH), tile),
        pl.BlockSpec((SUBLANES, GROUP_WIDTH), lambda i: (jnp.maximum(pos(i) * per8 - 1, 0), 0)),
        pl.BlockSpec((SUBLANES, GROUP_WIDTH), lambda i: (jnp.minimum((pos(i) + 1) * per8, n // SUBLANES - 1), 0)),
        pl.BlockSpec((CONV_W, GROUP_WIDTH), fixed),
        pl.BlockSpec((1, GROUP_WIDTH), fixed),
        pl.BlockSpec((LRU_BLOCKS, LRU_BLOCK, LRU_BLOCK), fixed3),
        pl.BlockSpec((1, GROUP_WIDTH), fixed),
        pl.BlockSpec((LRU_BLOCKS, LRU_BLOCK, LRU_BLOCK), fixed3),
        pl.BlockSpec((1, GROUP_WIDTH), fixed),
        pl.BlockSpec((1, GROUP_WIDTH), fixed),
        pl.BlockSpec((1, GROUP_WIDTH), fixed),
    ]
    args = [u_lru, u_lru, u_lru, conv_w, conv_b, wr, br, wi, bi, lam, h0]
    if final:
        in_specs += [pl.BlockSpec((tt, GROUP_WIDTH), lambda i: (pos(i), 1)),
                     pl.BlockSpec((tt, GROUP_WIDTH), tile)]
        args += [u_lru, h_fwd]
    return pl.pallas_call(
        functools.partial(_lru_kernel, reverse=reverse, final=final),
        grid=(nt,),
        in_specs=in_specs,
        out_specs=[pl.BlockSpec((tt, GROUP_WIDTH), tile), pl.BlockSpec((1, GROUP_WIDTH), fixed)],
        out_shape=[jax.ShapeDtypeStruct((n, GROUP_WIDTH), BF16 if final else F32),
                   jax.ShapeDtypeStruct((1, GROUP_WIDTH), F32)],
        scratch_shapes=[pltpu.VMEM((1, GROUP_WIDTH), F32)],
        compiler_params=_cparams("arbitrary"),
        name="rglru_bwd" if reverse else "rglru_fwd",
    )(*args)


def _dft_tables(n):
    k = np.arange(n, dtype=np.int64)
    ang = 2.0 * np.pi * ((k[:, None] * k[None, :]) % n).astype(np.float64) / n
    return np.cos(ang), np.sin(ang)


def _fft_small_kernel(g_ref, cs_ref, cn_ref, sn_ref, o_ref, *, norm):
    g = g_ref[...]
    a_parts, b_parts = [], []
    for grp in range(FFT_GROUPS):
        gg = g[:, grp * FFT_GROUP:(grp + 1) * FFT_GROUP]
        ab = jnp.dot(gg, cs_ref[...], preferred_element_type=F32)
        a_parts.append(ab[:, :FFT_GROUP])
        b_parts.append(ab[:, FFT_GROUP:])
    a = jnp.concatenate(a_parts, axis=1).astype(BF16)
    b = jnp.concatenate(b_parts, axis=1).astype(BF16)
    y = (jnp.dot(cn_ref[...], a, preferred_element_type=F32)
         - jnp.dot(sn_ref[...], b, preferred_element_type=F32))
    o_ref[...] = (y * norm).astype(o_ref.dtype)


def _fourier_small(g):
    n = g.shape[0]
    cc, sc = _dft_tables(FFT_GROUP)
    cn, sn = _dft_tables(n)
    cs = jnp.asarray(np.concatenate([cc, sc], axis=1), BF16)
    return pl.pallas_call(
        functools.partial(_fft_small_kernel, norm=float((n * FFT_GROUP) ** -0.5)),
        out_shape=jax.ShapeDtypeStruct((n, GROUP_WIDTH), BF16),
        compiler_params=pltpu.CompilerParams(vmem_limit_bytes=VMEM_LIMIT),
        name="fourier_ctx",
    )(g, cs, jnp.asarray(cn, BF16), jnp.asarray(sn, BF16))


def _fft_stage1_kernel(f_ref, x_ref, z_ref):
    res = jnp.dot(f_ref[...], x_ref[...], preferred_element_type=F32)
    tj = z_ref.shape[2]
    res = res.reshape(2, FFT_N1, tj * GROUP_WIDTH)
    for jj in range(tj):
        z_ref[:, :, jj, :] = res[:, :, jj * GROUP_WIDTH:(jj + 1) * GROUP_WIDTH]


def _fft_stage2_kernel(z_ref, twr_ref, twi_ref, f2_ref, cs_ref, o_ref, *, norm):
    tk1 = z_ref.shape[1]
    for i in range(tk1):
        zr = z_ref[0, i]
        zi = z_ref[1, i]
        twr = jnp.concatenate([twr_ref[i]] * (GROUP_WIDTH // LANES), axis=1)
        twi = jnp.concatenate([twi_ref[i]] * (GROUP_WIDTH // LANES), axis=1)
        zz = jnp.concatenate([zr * twr - zi * twi, zr * twi + zi * twr], axis=0).astype(BF16)
        p = jnp.dot(f2_ref[...], zz, preferred_element_type=F32)
        n2 = p.shape[0] // 2
        pr = p[:n2].astype(BF16)
        pim = p[n2:].astype(BF16)
        outs = []
        for grp in range(FFT_GROUPS):
            sl = slice(grp * FFT_GROUP, (grp + 1) * FFT_GROUP)
            lhs = jnp.concatenate([pr[:, sl], pim[:, sl]], axis=1)
            outs.append(jnp.dot(lhs, cs_ref[...], preferred_element_type=F32))
        o_ref[:, i, :] = (jnp.concatenate(outs, axis=1) * norm).astype(o_ref.dtype)


def _fourier_long(g):
    n = g.shape[0]
    n1 = FFT_N1
    assert n % (n1 * SUBLANES) == 0, n
    n2 = n // n1
    c1, s1 = _dft_tables(n1)
    f1 = jnp.asarray(np.concatenate([c1, -s1], axis=0), BF16)
    tj = min(n2, SUBLANES)
    z = pl.pallas_call(
        _fft_stage1_kernel,
        grid=(n2 // tj,),
        in_specs=[pl.BlockSpec((2 * n1, n1), lambda j: (0, 0)),
                  pl.BlockSpec((n1, tj * GROUP_WIDTH), lambda j: (0, j))],
        out_specs=pl.BlockSpec((2, n1, tj, GROUP_WIDTH), lambda j: (0, 0, j, 0)),
        out_shape=jax.ShapeDtypeStruct((2, n1, n2, GROUP_WIDTH), F32),
        compiler_params=_cparams("arbitrary"),
        name="fourier_stage1",
    )(f1, g.reshape(n1, n2 * GROUP_WIDTH))

    k1 = jnp.arange(n1, dtype=F32)[:, None]
    j2 = jnp.arange(n2, dtype=F32)[None, :]
    ang = (2.0 * np.pi / n) * (k1 * j2)
    twr = jnp.broadcast_to(jnp.cos(ang)[:, :, None], (n1, n2, LANES))
    twi = jnp.broadcast_to(-jnp.sin(ang)[:, :, None], (n1, n2, LANES))
    c2, s2 = _dft_tables(n2)
    f2 = jnp.asarray(np.block([[c2, s2], [-s2, c2]]), BF16)
    cc, sc = _dft_tables(FFT_GROUP)
    cs = jnp.asarray(np.concatenate([cc, sc], axis=0), BF16)
    tk1 = SUBLANES
    out = pl.pallas_call(
        functools.partial(_fft_stage2_kernel, norm=float((n * FFT_GROUP) ** -0.5)),
        grid=(n1 // tk1,),
        in_specs=[pl.BlockSpec((2, tk1, n2, GROUP_WIDTH), lambda i: (0, i, 0, 0)),
                  pl.BlockSpec((tk1, n2, LANES), lambda i: (i, 0, 0)),
                  pl.BlockSpec((tk1, n2, LANES), lambda i: (i, 0, 0)),
                  pl.BlockSpec((2 * n2, 2 * n2), lambda i: (0, 0)),
                  pl.BlockSpec((2 * FFT_GROUP, FFT_GROUP), lambda i: (0, 0))],
        out_specs=pl.BlockSpec((n2, tk1, GROUP_WIDTH), lambda i: (0, i, 0)),
        out_shape=jax.ShapeDtypeStruct((n2, n1, GROUP_WIDTH), BF16),
        compiler_params=_cparams("arbitrary"),
        name="fourier_stage2",
    )(z, twr, twi, f2, cs)
    return out.reshape(n, GROUP_WIDTH)


def _layernorm(z, g, b):
    mu = jnp.mean(z, axis=-1, keepdims=True)
    zc = z - mu
    var = jnp.mean(zc * zc, axis=-1, keepdims=True)
    return zc * lax.rsqrt(var + LN_EPS) * g + b


def _outproj_kernel(da_ref, lru_ref, mla_ref, fft_ref, x_ref, gate_ref, w_ref, g_ref, b_ref, o_ref, *, alpha):
    y = jnp.dot(da_ref[...], w_ref[0:GROUP_WIDTH], preferred_element_type=F32)
    y += jnp.dot(lru_ref[...], w_ref[GROUP_WIDTH:2 * GROUP_WIDTH], preferred_element_type=F32)
    y += jnp.dot(mla_ref[...], w_ref[2 * GROUP_WIDTH:3 * GROUP_WIDTH], preferred_element_type=F32)
    y += jnp.dot(fft_ref[...], w_ref[3 * GROUP_WIDTH:4 * GROUP_WIDTH], preferred_element_type=F32)
    z = alpha * x_ref[...] + gate_ref[...] * y
    o_ref[...] = _layernorm(z, g_ref[...], b_ref[...])


def _outproj_ln(parts, x, gate, w_out, ln_g, ln_b, alpha, tm):
    n, d = x.shape
    tm = _row_tile(n, tm)
    row = lambda i: (i, 0)
    fixed = lambda i: (0, 0)
    return pl.pallas_call(
        functools.partial(_outproj_kernel, alpha=alpha),
        grid=(n // tm,),
        in_specs=[pl.BlockSpec((tm, GROUP_WIDTH), row)] * 4 + [
            pl.BlockSpec((tm, d), row),
            pl.BlockSpec((1, d), fixed),
            pl.BlockSpec(w_out.shape, fixed),
            pl.BlockSpec((1, d), fixed),
            pl.BlockSpec((1, d), fixed),
        ],
        out_specs=pl.BlockSpec((tm, d), row),
        out_shape=jax.ShapeDtypeStruct((n, d), F32),
        compiler_params=_cparams("arbitrary"),
        name="out_proj_ln",
    )(*parts, x, gate, w_out, ln_g, ln_b)


def _ffn_kernel(x_ref, sh_ref, sc_ref, gate_ref, wg_ref, wu_ref, wd_ref, g_ref, b_ref, o_ref,
                h_ref, acc_ref, *, alpha):
    f = pl.program_id(1)

    @pl.when(f == 0)
    def _():
        h_ref[...] = (x_ref[...] * (1.0 + sc_ref[...]) + sh_ref[...]).astype(BF16)
        acc_ref[...] = jnp.zeros(acc_ref.shape, F32)

    h = h_ref[...]
    a = jnp.dot(h, wg_ref[...], preferred_element_type=F32)
    u = jnp.dot(h, wu_ref[...], preferred_element_type=F32)
    act = (a * jax.nn.sigmoid(a) * u).astype(BF16)
    acc_ref[...] += jnp.dot(act, wd_ref[...], preferred_element_type=F32)

    @pl.when(f == pl.num_programs(1) - 1)
    def _():
        z = alpha * x_ref[...] + gate_ref[...] * acc_ref[...]
        o_ref[...] = _layernorm(z, g_ref[...], b_ref[...])


def _ffn_ln(x, shift, scale, gate, wg, wu, wd, ln_g, ln_b, alpha, tm, tf):
    n, d = x.shape
    d_ff = wg.shape[1]
    tm = _row_tile(n, tm)
    assert d_ff % tf == 0
    row = lambda i, f: (i, 0)
    fixed = lambda i, f: (0, 0)
    return pl.pallas_call(
        functools.partial(_ffn_kernel, alpha=alpha),
        grid=(n // tm, d_ff // tf),
        in_specs=[
            pl.BlockSpec((tm, d), row),
            pl.BlockSpec((1, d), fixed),
            pl.BlockSpec((1, d), fixed),
            pl.BlockSpec((1, d), fixed),
            pl.BlockSpec((d, tf), lambda i, f: (0, f)),
            pl.BlockSpec((d, tf), lambda i, f: (0, f)),
            pl.BlockSpec((tf, d), lambda i, f: (f, 0)),
            pl.BlockSpec((1, d), fixed),
            pl.BlockSpec((1, d), fixed),
        ],
        out_specs=pl.BlockSpec((tm, d), row),
        out_shape=jax.ShapeDtypeStruct((n, d), F32),
        scratch_shapes=[pltpu.VMEM((tm, d), BF16), pltpu.VMEM((tm, d), F32)],
        compiler_params=_cparams("arbitrary", "arbitrary"),
        name="ffn_ln",
    )(x, shift, scale, gate, wg, wu, wd, ln_g, ln_b)


def _pad_in_weight(w_in):
    d = w_in.shape[0]
    src_fft = 3 * GROUP_WIDTH + 2 * GROUP_WIDTH + MLA_Q_RANK + MLA_KV_RANK + MLA_ROPE
    pad = jnp.zeros((d, SEG_FFT - (SEG_MLA + MLA_Q_RANK + MLA_KV_RANK + MLA_ROPE)), w_in.dtype)
    return jnp.concatenate([w_in[:, :src_fft], pad, w_in[:, src_fft:]], axis=1).astype(BF16)


def _pad_mla_weights(w_uq, w_ukv):
    qr = w_uq.shape[0]
    wq = w_uq.reshape(qr, HEADS, MLA_NOPE + MLA_ROPE)
    wq = jnp.concatenate([wq, jnp.zeros((qr, HEADS, MLA_HEAD_PAD - MLA_NOPE - MLA_ROPE), w_uq.dtype)], axis=2)
    wkv = w_ukv.reshape(w_ukv.shape[0], HEADS, MLA_NOPE + MLA_V)
    wk = wkv[:, :, :MLA_NOPE].reshape(w_ukv.shape[0], HEADS * MLA_NOPE)
    wv = wkv[:, :, MLA_NOPE:].reshape(w_ukv.shape[0], HEADS * MLA_V)
    return (wq.reshape(qr, HEADS * MLA_HEAD_PAD).astype(BF16), wk.astype(BF16), wv.astype(BF16))


def _rope_tables(n):
    rows = n // GRID_W
    row = jnp.repeat(jnp.arange(rows, dtype=F32), GRID_W)
    col = jnp.tile(jnp.arange(GRID_W, dtype=F32), rows)
    axis_dim = ROT_DIM // 2
    inv = ROPE_BASE ** (-jnp.arange(0, axis_dim, 2, dtype=F32) / axis_dim)
    ang = jnp.concatenate([row[:, None] * inv, col[:, None] * inv], axis=-1)
    cos, sin = jnp.cos(ang), jnp.sin(ang)
    cos_slab = jnp.concatenate([cos, cos] * (LANES // ROT_DIM), axis=1)
    sin_slab = jnp.concatenate([-sin, sin] * (LANES // ROT_DIM), axis=1)
    return cos_slab, sin_slab


def kernel(x, c, ctx, c_ctx, w_ada, b_ada, w_in, w_out, ln1_g, ln1_b, ln2_g, ln2_b,
           da_lq1, da_lk1, da_lq2, da_lk2, da_subln_g,
           lru_conv_w, lru_conv_b, lru_wr, lru_br, lru_wi, lru_bi, lru_lam,
           mla_qn_g, mla_wuq, mla_kvn_g, mla_wukv,
           ffn_wg, ffn_wu, ffn_wd):
    assert x.shape[0] == 1 and c.shape[0] == 1 and ctx.shape[0] == 1
    depth = w_ada.shape[0]
    n, d = x.shape[1], x.shape[2]
    nc = ctx.shape[1]
    alpha = (2 * depth) ** 0.25
    x_lat, x_ctx = x[0], ctx[0]

    cos_l, sin_l = _rope_tables(n)
    cos_c, sin_c = jnp.ones((nc, LANES), F32), jnp.zeros((nc, LANES), F32)

    c8 = jnp.zeros((SUBLANES, d), F32).at[0].set(c[0]).at[1].set(c_ctx)
    mods = _ada_mods(c8, w_ada, b_ada)

    row2 = lambda v: v.reshape(1, -1)
    for l in range(depth):
        need_ctx = l < depth - 1
        lambda_init = 0.8 - 0.6 * math.exp(-0.3 * l)
        m_lat = [mods[l, 0:1, i * d:(i + 1) * d] for i in range(6)]
        m_ctx = [mods[l, 1:2, i * d:(i + 1) * d] for i in range(6)]

        w_in_p = _pad_in_weight(w_in[l])
        w_out_b = w_out[l].astype(BF16)
        wq_p, wk_p, wv_p = _pad_mla_weights(mla_wuq[l], mla_wukv[l])
        wg_b, wu_b, wd_b = ffn_wg[l].astype(BF16), ffn_wu[l].astype(BF16), ffn_wd[l].astype(BF16)

        ql, kl, vl, lru_l, mla_l, fft_l = _inproj(x_lat, m_lat[0], m_lat[1], cos_l, sin_l, w_in_p, 512)
        qc, kc, vc, lru_c, mla_c, fft_c = _inproj(x_ctx, m_ctx[0], m_ctx[1], cos_c, sin_c, w_in_p, 256)

        lams = [row2(da_lq1[l]), row2(da_lk1[l]), row2(da_lq2[l]), row2(da_lk2[l])]
        g_da = row2(da_subln_g[l])
        da_l = _diff_attention(ql, kl, vl, (kc, vc), lams, g_da, lambda_init, ATTN_TQ, ATTN_TK, ATTN_CHUNK)

        lru_args = lambda dr: (lru_conv_w[l], row2(lru_conv_b[l]), lru_wr[l, dr].astype(BF16), row2(lru_br[l, dr]),
                               lru_wi[l, dr].astype(BF16), row2(lru_bi[l, dr]), row2(lru_lam[l, dr]))
        h0 = jnp.zeros((1, GROUP_WIDTH), F32)
        hc_f, s_f = _lru_pass(lru_c, *lru_args(0), h0, None, False, 256)
        lru_yc, s_b = _lru_pass(lru_c, *lru_args(1), h0, hc_f, True, 256)
        hl_f, _ = _lru_pass(lru_l, *lru_args(0), s_f, None, False, 256)
        lru_yl, _ = _lru_pass(lru_l, *lru_args(1), s_b, hl_f, True, 256)

        mq_l, mk_l, mv_l = _mla_prep(mla_l, cos_l, sin_l, row2(mla_qn_g[l]), row2(mla_kvn_g[l]), wq_p, wk_p, wv_p, 512)
        mq_c, mk_c, mv_c = _mla_prep(mla_c, cos_c, sin_c, row2(mla_qn_g[l]), row2(mla_kvn_g[l]), wq_p, wk_p, wv_p, 256)
        mla_yl = _mla_attention(mq_l, mk_l, mv_l, (mk_c, mv_c), ATTN_TQ, ATTN_TK, ATTN_CHUNK)

        fft_yl = _fourier_long(fft_l)

        x_lat = _outproj_ln([da_l, lru_yl, mla_yl, fft_yl], x_lat, m_lat[2], w_out_b,
                            row2(ln1_g[l]), row2(ln1_b[l]), alpha, 512)
        x_lat = _ffn_ln(x_lat, m_lat[3], m_lat[4], m_lat[5], wg_b, wu_b, wd_b,
                        row2(ln2_g[l]), row2(ln2_b[l]), alpha, 512, 512)

        if need_ctx:
            da_c = _diff_attention(qc, kc, vc, None, lams, g_da, lambda_init, 256, 256, 256)
            mla_yc = _mla_attention(mq_c, mk_c, mv_c, None, 256, 256, 256)
            fft_yc = _fourier_small(fft_c)
            x_ctx = _outproj_ln([da_c, lru_yc, mla_yc, fft_yc], x_ctx, m_ctx[2], w_out_b,
                                row2(ln1_g[l]), row2(ln1_b[l]), alpha, 256)
            x_ctx = _ffn_ln(x_ctx, m_ctx[3], m_ctx[4], m_ctx[5], wg_b, wu_b, wd_b,
                            row2(ln2_g[l]), row2(ln2_b[l]), alpha, 256, 512)
    return x_lat[None]
```

```python
import functools
import math

import numpy as np
import jax
import jax.numpy as jnp
from jax import lax
from jax.experimental import pallas as pl
from jax.experimental.pallas import tpu as pltpu

F32 = jnp.float32
BF16 = jnp.bfloat16

D_MODEL = 2048
GRID_W = 64
ROPE_BASE = 10000.0
LN_EPS = 1e-5
RMS_EPS = 1e-6

HEADS = 4
DA_HEAD_DIM = 64
GROUP_WIDTH = 512
LRU_BLOCKS = 4
LRU_BLOCK = GROUP_WIDTH // LRU_BLOCKS
CONV_W = 4
CONV_LEFT = 2
LRU_C = 8.0
MLA_Q_RANK = 384
MLA_KV_RANK = 256
MLA_NOPE = 128
MLA_ROPE = 64
MLA_V = 128
FFT_GROUPS = 4
FFT_GROUP = 128
ROT_DIM = 64
ROT_HALF = ROT_DIM // 2

LANES = 128
SUBLANES = 8
SEG_Q = 0
SEG_K = 512
SEG_V = 1024
SEG_LRU = 1536
SEG_MLA = 2560
SEG_FFT = 3328
IN_PAD = 3840
MLA_SEG = 768
MLA_HEAD_PAD = 256

FFT_N1 = 128

VMEM_LIMIT = 56 * 1024 * 1024
LOG2E = math.log2(math.e)

ATTN_TQ = 1024
ATTN_TK = 4096
ATTN_CHUNK = 512
SOFTMAX_SLAB = 64
ONES_ROWS = 16
MAX_EXCESS = 32.0


def _cparams(*sem):
    return pltpu.CompilerParams(dimension_semantics=sem, vmem_limit_bytes=VMEM_LIMIT)


def _row_tile(n, want):
    t = min(n, want)
    assert n % t == 0, (n, t)
    return t


def _ada_kernel(c_ref, w_ref, b_ref, o_ref):
    s = c_ref[...]
    s = s * jax.nn.sigmoid(s)
    o_ref[...] = jnp.dot(s.astype(BF16), w_ref[...].astype(BF16), preferred_element_type=F32) + b_ref[...]


def _ada_mods(c8, w_ada, b_ada):
    depth, d, n6 = w_ada.shape
    tn = 1536
    return pl.pallas_call(
        _ada_kernel,
        grid=(depth, n6 // tn),
        in_specs=[
            pl.BlockSpec((SUBLANES, d), lambda l, j: (0, 0)),
            pl.BlockSpec((None, d, tn), lambda l, j: (l, 0, j)),
            pl.BlockSpec((None, 1, tn), lambda l, j: (l, 0, j)),
        ],
        out_specs=pl.BlockSpec((None, SUBLANES, tn), lambda l, j: (l, 0, j)),
        out_shape=jax.ShapeDtypeStruct((depth, SUBLANES, n6), F32),
        compiler_params=_cparams("arbitrary", "arbitrary"),
        name="ada_mods",
    )(c8, w_ada, b_ada.reshape(depth, 1, n6))


def _rope_slab(x, cos, sin_signed):
    lane = lax.broadcasted_iota(jnp.int32, x.shape, 1)
    partner = jnp.where((lane & ROT_HALF) == 0,
                        pltpu.roll(x, LANES - ROT_HALF, 1),
                        pltpu.roll(x, ROT_HALF, 1))
    return x * cos + partner * sin_signed


def _inproj_kernel(x_ref, sh_ref, sc_ref, cos_ref, sin_ref, w_ref,
                   q_ref, k_ref, v_ref, lru_ref, mla_ref, fft_ref, *, q_scale):
    xm = (x_ref[...] * (1.0 + sc_ref[...]) + sh_ref[...]).astype(BF16)
    cos = cos_ref[...]
    sin = sin_ref[...]

    def seg(a, width):
        return jnp.dot(xm, w_ref[:, a:a + width], preferred_element_type=F32)

    q_all = seg(SEG_Q, GROUP_WIDTH)
    k_all = seg(SEG_K, GROUP_WIDTH)
    for h in range(HEADS):
        o = h * LANES
        q_ref[:, o:o + LANES] = (_rope_slab(q_all[:, o:o + LANES], cos, sin) * q_scale).astype(BF16)
        k_ref[:, o:o + LANES] = _rope_slab(k_all[:, o:o + LANES], cos, sin).astype(BF16)
    v_ref[...] = seg(SEG_V, GROUP_WIDTH).T.astype(BF16)
    lru_ref[...] = seg(SEG_LRU, 2 * GROUP_WIDTH)
    mla_ref[...] = seg(SEG_MLA, MLA_SEG)
    fft_ref[...] = seg(SEG_FFT, GROUP_WIDTH).astype(BF16)


def _inproj(x, shift, scale, cos, sin, w_in_p, layer, tm):
    n, d = x.shape
    tm = _row_tile(n, tm)
    row = lambda i: (i, 0)
    col = lambda i: (0, i)
    fixed = lambda i: (0, 0)
    outs = [
        jax.ShapeDtypeStruct((n, GROUP_WIDTH), BF16),
        jax.ShapeDtypeStruct((n, GROUP_WIDTH), BF16),
        jax.ShapeDtypeStruct((GROUP_WIDTH, n), BF16),
        jax.ShapeDtypeStruct((n, 2 * GROUP_WIDTH), F32),
        jax.ShapeDtypeStruct((n, MLA_SEG), F32),
        jax.ShapeDtypeStruct((n, GROUP_WIDTH), BF16),
    ]
    out_specs = [pl.BlockSpec((tm, o.shape[1]), row) for o in outs]
    out_specs[2] = pl.BlockSpec((GROUP_WIDTH, tm), col)
    return pl.pallas_call(
        functools.partial(_inproj_kernel, q_scale=DA_HEAD_DIM ** -0.5 * LOG2E),
        grid=(n // tm,),
        in_specs=[
            pl.BlockSpec((tm, d), row),
            pl.BlockSpec((1, d), fixed),
            pl.BlockSpec((1, d), fixed),
            pl.BlockSpec((tm, LANES), row),
            pl.BlockSpec((tm, LANES), row),
            pl.BlockSpec((None, d, IN_PAD), lambda i: (layer, 0, 0)),
        ],
        out_specs=out_specs,
        out_shape=outs,
        compiler_params=_cparams("arbitrary"),
        name="in_proj",
    )(x, shift, scale, cos, sin, w_in_p)


def _scores_t(q_t, k):
    return jnp.dot(k, q_t, preferred_element_type=F32)


def _probs_t(s_t, ref_max):
    keys = s_t.shape[0]
    slab = min(keys, SOFTMAX_SLAB)
    return jnp.concatenate([jnp.exp2(s_t[r:r + slab] - ref_max).astype(BF16) for r in range(0, keys, slab)], axis=0)


def _weighted_values_t(v_t, p_t):
    v_ext = jnp.concatenate([v_t, jnp.ones((ONES_ROWS, v_t.shape[1]), BF16)], axis=0)
    return jnp.dot(v_ext, p_t, preferred_element_type=F32)


def _softmax_update(s_t, v_t, m_ref, acc_ref, c):
    m_prev = m_ref[c]
    m_new = jnp.maximum(m_prev, jnp.max(s_t, axis=0, keepdims=True))
    alpha = jnp.exp2(m_prev - m_new)
    acc_ref[c] = alpha * acc_ref[c] + _weighted_values_t(v_t, _probs_t(s_t, m_new))
    m_ref[c] = m_new


def _flash_kernel(*refs, n_comp, has_prefix, chunk, epilogue):
    refs = list(refs)
    q_ref = refs.pop(0)
    if has_prefix:
        kc_ref, vtc_ref = refs.pop(0), refs.pop(0)
    k_ref, vt_ref = refs.pop(0), refs.pop(0)
    qt_ref, m_ref, acc_ref = refs[-3:]
    o_ref = refs[-4]
    extra = refs[:-4]
    j = pl.program_id(2)

    @pl.when(j == 0)
    def _():
        m_ref[...] = jnp.full(m_ref.shape, -jnp.inf, F32)
        acc_ref[...] = jnp.zeros(acc_ref.shape, F32)
        q_t = q_ref[...].astype(F32).T
        if n_comp == 2:
            row = lax.broadcasted_iota(jnp.int32, q_t.shape, 0)
            qt_ref[0] = jnp.where(row < DA_HEAD_DIM, q_t, 0.0).astype(BF16)
            qt_ref[1] = jnp.where(row >= DA_HEAD_DIM, q_t, 0.0).astype(BF16)
        else:
            qt_ref[0] = q_t.astype(BF16)
        if has_prefix:
            for c in range(n_comp):
                _softmax_update(_scores_t(qt_ref[c], kc_ref[...]), vtc_ref[...], m_ref, acc_ref, c)

    n_chunks = k_ref.shape[0] // chunk

    def exact_step():
        def body(i, carry):
            off = pl.multiple_of(i * chunk, chunk)
            for c in range(n_comp):
                _softmax_update(_scores_t(qt_ref[c], k_ref[pl.ds(off, chunk), :]), vt_ref[:, pl.ds(off, chunk)],
                                m_ref, acc_ref, c)
            return carry
        lax.fori_loop(0, n_chunks, body, 0)

    if not has_prefix:
        exact_step()
    else:
        ref_max = [m_ref[c] for c in range(n_comp)]
        step_acc = [None] * n_comp
        step_max = [None] * n_comp
        for i in range(n_chunks):
            off = i * chunk
            for c in range(n_comp):
                s_t = _scores_t(qt_ref[c], k_ref[off:off + chunk, :])
                unit_max = jnp.max(s_t, axis=0, keepdims=True)
                pv = _weighted_values_t(vt_ref[:, off:off + chunk], _probs_t(s_t, ref_max[c]))
                step_acc[c] = pv if step_acc[c] is None else step_acc[c] + pv
                step_max[c] = unit_max if step_max[c] is None else jnp.maximum(step_max[c], unit_max)
        excess = jnp.max(step_max[0] - ref_max[0])
        for c in range(1, n_comp):
            excess = jnp.maximum(excess, jnp.max(step_max[c] - ref_max[c]))
        in_range = excess <= MAX_EXCESS

        @pl.when(in_range)
        def _():
            for c in range(n_comp):
                m_new = jnp.maximum(ref_max[c], step_max[c])
                acc_ref[c] = (acc_ref[c] + step_acc[c]) * jnp.exp2(ref_max[c] - m_new)
                m_ref[c] = m_new

        @pl.when(jnp.logical_not(in_range))
        def _():
            exact_step()

    @pl.when(j == pl.num_programs(2) - 1)
    def _():
        epilogue(extra, o_ref, acc_ref)


def _normalised_t(acc_ref, c):
    dv = acc_ref.shape[1] - ONES_ROWS
    return acc_ref[c, 0:dv, :] / acc_ref[c, dv:dv + 1, :]


def _da_epilogue(extra, o_ref, acc_ref, *, lambda_init):
    lq1_ref, lk1_ref, lq2_ref, lk2_ref, g_ref = extra
    lam = (jnp.exp(jnp.sum(lq1_ref[...] * lk1_ref[...], keepdims=True))
           - jnp.exp(jnp.sum(lq2_ref[...] * lk2_ref[...], keepdims=True)) + lambda_init)
    o = (_normalised_t(acc_ref, 0) - lam * _normalised_t(acc_ref, 1)).T
    inv = lax.rsqrt(jnp.mean(o * o, axis=-1, keepdims=True) + RMS_EPS)
    o_ref[...] = ((o * inv * g_ref[...]) * (1.0 - lambda_init)).astype(o_ref.dtype)


def _mla_epilogue(extra, o_ref, acc_ref):
    o_ref[...] = _normalised_t(acc_ref, 0).T.astype(o_ref.dtype)


def _flash_attention(q, k, v_t, prefix, extra, extra_specs, *, dk, n_comp, epilogue, tq, tk, chunk, name):
    nq, nk = q.shape[0], k.shape[0]
    dv = v_t.shape[0] // HEADS
    tq, tk = _row_tile(nq, tq), _row_tile(nk, tk)
    chunk = _row_tile(tk, chunk)
    has_prefix = prefix is not None
    qspec = pl.BlockSpec((tq, dk), lambda h, i, j: (i, h))
    in_specs, args = [qspec], [q]
    if has_prefix:
        nc = prefix[0].shape[0]
        in_specs += [pl.BlockSpec((nc, dk), lambda h, i, j: (0, h)),
                     pl.BlockSpec((dv, nc), lambda h, i, j: (h, 0))]
        args += list(prefix)
    in_specs += [pl.BlockSpec((tk, dk), lambda h, i, j: (j, h)),
                 pl.BlockSpec((dv, tk), lambda h, i, j: (h, j))] + list(extra_specs)
    args += [k, v_t] + list(extra)
    return pl.pallas_call(
        functools.partial(_flash_kernel, n_comp=n_comp, has_prefix=has_prefix, chunk=chunk, epilogue=epilogue),
        grid=(HEADS, nq // tq, nk // tk),
        in_specs=in_specs,
        out_specs=pl.BlockSpec((tq, dv), lambda h, i, j: (i, h)),
        out_shape=jax.ShapeDtypeStruct((nq, HEADS * dv), BF16),
        scratch_shapes=[
            pltpu.VMEM((n_comp, dk, tq), BF16),
            pltpu.VMEM((n_comp, 1, tq), F32),
            pltpu.VMEM((n_comp, dv + ONES_ROWS, tq), F32),
        ],
        compiler_params=_cparams("arbitrary", "arbitrary", "arbitrary"),
        name=name,
    )(*args)


def _diff_attention(q, k, v_t, prefix, lams, g, lambda_init, tq, tk, chunk):
    small = lambda w: pl.BlockSpec((1, w), lambda h, i, j: (0, 0))
    return _flash_attention(
        q, k, v_t, prefix, list(lams) + [g], [small(DA_HEAD_DIM)] * 4 + [small(LANES)],
        dk=2 * DA_HEAD_DIM, n_comp=2, epilogue=functools.partial(_da_epilogue, lambda_init=lambda_init),
        tq=tq, tk=tk, chunk=chunk, name="diff_attention")


def _mla_attention(q, k, v_t, prefix, tq, tk, chunk):
    return _flash_attention(q, k, v_t, prefix, [], [], dk=MLA_HEAD_PAD, n_comp=1, epilogue=_mla_epilogue,
                            tq=tq, tk=tk, chunk=chunk, name="mla_attention")


def _rms(x, g):
    inv = lax.rsqrt(jnp.mean(x * x, axis=-1, keepdims=True) + RMS_EPS)
    return x * inv * g


def _mla_prep_kernel(u_ref, cos_ref, sin_ref, qg_ref, kvg_ref, wq_ref, wk_ref, wv_ref,
                     q_ref, k_ref, v_ref, *, scale):
    cos = cos_ref[...]
    sin = sin_ref[...]
    cq = _rms(u_ref[:, 0:MLA_Q_RANK], qg_ref[...]).astype(BF16)
    ckv = _rms(u_ref[:, MLA_Q_RANK:MLA_Q_RANK + MLA_KV_RANK], kvg_ref[...]).astype(BF16)
    k_rope = _rope_slab(u_ref[:, MLA_Q_RANK + MLA_KV_RANK:MLA_SEG], cos, sin).astype(BF16)
    k_nope = jnp.dot(ckv, wk_ref[...], preferred_element_type=F32)
    for h in range(HEADS):
        o = h * MLA_HEAD_PAD
        q_h = jnp.dot(cq, wq_ref[:, o:o + MLA_HEAD_PAD], preferred_element_type=F32)
        q_ref[:, o:o + LANES] = (q_h[:, :LANES] * scale).astype(BF16)
        q_ref[:, o + LANES:o + 2 * LANES] = (_rope_slab(q_h[:, LANES:], cos, sin) * scale).astype(BF16)
        k_ref[:, o:o + LANES] = k_nope[:, h * LANES:(h + 1) * LANES].astype(BF16)
        k_ref[:, o + LANES:o + 2 * LANES] = k_rope
    v_ref[...] = jnp.dot(ckv, wv_ref[...], preferred_element_type=F32).T.astype(BF16)


def _mla_prep(u_mla, cos, sin, qn_g, kvn_g, wq_p, wk_p, wv_p, tm):
    n = u_mla.shape[0]
    tm = _row_tile(n, tm)
    row = lambda i: (i, 0)
    fixed = lambda i: (0, 0)
    outs = [
        jax.ShapeDtypeStruct((n, HEADS * MLA_HEAD_PAD), BF16),
        jax.ShapeDtypeStruct((n, HEADS * MLA_HEAD_PAD), BF16),
        jax.ShapeDtypeStruct((GROUP_WIDTH, n), BF16),
    ]
    out_specs = [pl.BlockSpec((tm, HEADS * MLA_HEAD_PAD), row)] * 2 + [pl.BlockSpec((GROUP_WIDTH, tm), lambda i: (0, i))]
    return pl.pallas_call(
        functools.partial(_mla_prep_kernel, scale=(MLA_NOPE + MLA_ROPE) ** -0.5 * LOG2E),
        grid=(n // tm,),
        in_specs=[
            pl.BlockSpec((tm, MLA_SEG), row),
            pl.BlockSpec((tm, LANES), row),
            pl.BlockSpec((tm, LANES), row),
            pl.BlockSpec((1, MLA_Q_RANK), fixed),
            pl.BlockSpec((1, MLA_KV_RANK), fixed),
            pl.BlockSpec(wq_p.shape, fixed),
            pl.BlockSpec(wk_p.shape, fixed),
            pl.BlockSpec(wv_p.shape, fixed),
        ],
        out_specs=out_specs,
        out_shape=outs,
        compiler_params=_cparams("arbitrary"),
        name="mla_prep",
    )(u_mla, cos, sin, qn_g, kvn_g, wq_p, wk_p, wv_p)


def _shift_rows(x, d, halo):
    tt = x.shape[0]
    row8 = lax.broadcasted_iota(jnp.int32, (SUBLANES, x.shape[1]), 0)
    if d > 0:
        r = pltpu.roll(x, d, 0)
        f = pltpu.roll(halo, d, 0)
        first = jnp.where(row8 < d, f, r[:SUBLANES])
        return jnp.concatenate([first, r[SUBLANES:]], axis=0) if tt > SUBLANES else first
    r = pltpu.roll(x, tt + d, 0)
    f = pltpu.roll(halo, SUBLANES + d, 0)
    last = jnp.where(row8 >= SUBLANES + d, f, r[tt - SUBLANES:])
    return jnp.concatenate([r[:tt - SUBLANES], last], axis=0) if tt > SUBLANES else last


def _lru_kernel(*refs, reverse, final):
    if final:
        (x_ref, prev_ref, next_ref, cw_ref, cb_ref, wr_ref, br_ref, wi_ref, bi_ref, lam_ref, h0_ref,
         gate_ref, hf_ref, y_ref, hlast_ref, carry_ref) = refs
    else:
        (x_ref, prev_ref, next_ref, cw_ref, cb_ref, wr_ref, br_ref, wi_ref, bi_ref, lam_ref, h0_ref,
         y_ref, hlast_ref, carry_ref) = refs
    i = pl.program_id(0)
    nt = pl.num_programs(0)
    t = (nt - 1 - i) if reverse else i
    x = x_ref[...]
    tt = x.shape[0]

    @pl.when(i == 0)
    def _():
        carry_ref[...] = h0_ref[...]

    prev = prev_ref[...] * (t > 0).astype(F32)
    nxt = next_ref[...] * (t < nt - 1).astype(F32)
    cw = cw_ref[...]
    xc = (cw[0:1] * _shift_rows(x, 2, prev) + cw[1:2] * _shift_rows(x, 1, prev)
          + cw[2:3] * x + cw[3:4] * _shift_rows(x, -1, nxt)) + cb_ref[...]

    xb = xc.astype(BF16)

    def gate(w_ref, b_ref):
        z = jnp.concatenate(
            [jnp.dot(xb[:, b * LRU_BLOCK:(b + 1) * LRU_BLOCK], w_ref[b], preferred_element_type=F32)
             for b in range(LRU_BLOCKS)], axis=1)
        return jax.nn.sigmoid(z + b_ref[...])

    r = gate(wr_ref, br_ref)
    ig = gate(wi_ref, bi_ref)
    log_a = -LRU_C * r * jax.nn.softplus(-lam_ref[...])
    a = jnp.exp(log_a)
    u = jnp.sqrt(1.0 - jnp.exp(2.0 * log_a)) * (ig * xc)

    row = lax.broadcasted_iota(jnp.int32, a.shape, 0)
    big_a, big_b = a, u
    d = 1
    while d < tt:
        if reverse:
            valid = row < tt - d
            a_s = pltpu.roll(big_a, tt - d, 0)
            b_s = pltpu.roll(big_b, tt - d, 0)
        else:
            valid = row >= d
            a_s = pltpu.roll(big_a, d, 0)
            b_s = pltpu.roll(big_b, d, 0)
        big_b = jnp.where(valid, big_a * b_s + big_b, big_b)
        big_a = jnp.where(valid, big_a * a_s, big_a)
        d *= 2
    h = big_a * carry_ref[...] + big_b
    edge = h[0:1] if reverse else h[tt - 1:tt]
    carry_ref[...] = edge
    hlast_ref[...] = edge
    if final:
        y_ref[...] = ((hf_ref[...] + h) * jax.nn.gelu(gate_ref[...])).astype(y_ref.dtype)
    else:
        y_ref[...] = h


def _lru_pass(u_lru, conv_w, conv_b, wr, br, wi, bi, lam, h0, h_fwd, reverse, tt):
    n = u_lru.shape[0]
    tt = _row_tile(n, tt)
    nt = n // tt
    per8 = tt // SUBLANES
    final = h_fwd is not None
    pos = (lambda i: nt - 1 - i) if reverse else (lambda i: i)
    tile = lambda i: (pos(i), 0)
    fixed = lambda i: (0, 0)
    fixed3 = lambda i: (0, 0, 0)
    in_specs = [
        pl.BlockSpec((tt, GROUP_WIDTH), tile),
        pl.BlockSpec((SUBLANES, GROUP_WIDTH), lambda i: (jnp.maximum(pos(i) * per8 - 1, 0), 0)),
        pl.BlockSpec((SUBLANES, GROUP_WIDTH), lambda i: (jnp.minimum((pos(i) + 1) * per8, n // SUBLANES - 1), 0)),
        pl.BlockSpec((CONV_W, GROUP_WIDTH), fixed),
        pl.BlockSpec((1, GROUP_WIDTH), fixed),
        pl.BlockSpec((LRU_BLOCKS, LRU_BLOCK, LRU_BLOCK), fixed3),
        pl.BlockSpec((1, GROUP_WIDTH), fixed),
        pl.BlockSpec((LRU_BLOCKS, LRU_BLOCK, LRU_BLOCK), fixed3),
        pl.BlockSpec((1, GROUP_WIDTH), fixed),
        pl.BlockSpec((1, GROUP_WIDTH), fixed),
        pl.BlockSpec((1, GROUP_WIDTH), fixed),
    ]
    args = [u_lru, u_lru, u_lru, conv_w, conv_b, wr, br, wi, bi, lam, h0]
    if final:
        in_specs += [pl.BlockSpec((tt, GROUP_WIDTH), lambda i: (pos(i), 1)),
                     pl.BlockSpec((tt, GROUP_WIDTH), tile)]
        args += [u_lru, h_fwd]
    return pl.pallas_call(
        functools.partial(_lru_kernel, reverse=reverse, final=final),
        grid=(nt,),
        in_specs=in_specs,
        out_specs=[pl.BlockSpec((tt, GROUP_WIDTH), tile), pl.BlockSpec((1, GROUP_WIDTH), fixed)],
        out_shape=[jax.ShapeDtypeStruct((n, GROUP_WIDTH), BF16 if final else F32),
                   jax.ShapeDtypeStruct((1, GROUP_WIDTH), F32)],
        scratch_shapes=[pltpu.VMEM((1, GROUP_WIDTH), F32)],
        compiler_params=_cparams("arbitrary"),
        name="rglru_bwd" if reverse else "rglru_fwd",
    )(*args)


def _dft_tables(n):
    k = np.arange(n, dtype=np.int64)
    ang = 2.0 * np.pi * ((k[:, None] * k[None, :]) % n).astype(np.float64) / n
    return np.cos(ang), np.sin(ang)


def _fft_small_kernel(g_ref, cs_ref, cn_ref, sn_ref, o_ref, *, norm):
    g = g_ref[...]
    a_parts, b_parts = [], []
    for grp in range(FFT_GROUPS):
        gg = g[:, grp * FFT_GROUP:(grp + 1) * FFT_GROUP]
        ab = jnp.dot(gg, cs_ref[...], preferred_element_type=F32)
        a_parts.append(ab[:, :FFT_GROUP])
        b_parts.append(ab[:, FFT_GROUP:])
    a = jnp.concatenate(a_parts, axis=1).astype(BF16)
    b = jnp.concatenate(b_parts, axis=1).astype(BF16)
    y = (jnp.dot(cn_ref[...], a, preferred_element_type=F32)
         - jnp.dot(sn_ref[...], b, preferred_element_type=F32))
    o_ref[...] = (y * norm).astype(o_ref.dtype)


def _fourier_small(g):
    n = g.shape[0]
    cc, sc = _dft_tables(FFT_GROUP)
    cn, sn = _dft_tables(n)
    cs = jnp.asarray(np.concatenate([cc, sc], axis=1), BF16)
    return pl.pallas_call(
        functools.partial(_fft_small_kernel, norm=float((n * FFT_GROUP) ** -0.5)),
        out_shape=jax.ShapeDtypeStruct((n, GROUP_WIDTH), BF16),
        compiler_params=pltpu.CompilerParams(vmem_limit_bytes=VMEM_LIMIT),
        name="fourier_ctx",
    )(g, cs, jnp.asarray(cn, BF16), jnp.asarray(sn, BF16))


def _fft_stage1_kernel(f_ref, x_ref, z_ref):
    res = jnp.dot(f_ref[...], x_ref[...], preferred_element_type=F32)
    tj = z_ref.shape[2]
    res = res.reshape(2, FFT_N1, tj * GROUP_WIDTH)
    for jj in range(tj):
        z_ref[:, :, jj, :] = res[:, :, jj * GROUP_WIDTH:(jj + 1) * GROUP_WIDTH]


def _fft_stage2_kernel(z_ref, twr_ref, twi_ref, f2_ref, cs_ref, o_ref, *, norm):
    tk1 = z_ref.shape[1]
    for i in range(tk1):
        zr = z_ref[0, i]
        zi = z_ref[1, i]
        twr = jnp.concatenate([twr_ref[i]] * (GROUP_WIDTH // LANES), axis=1)
        twi = jnp.concatenate([twi_ref[i]] * (GROUP_WIDTH // LANES), axis=1)
        zz = jnp.concatenate([zr * twr - zi * twi, zr * twi + zi * twr], axis=0).astype(BF16)
        p = jnp.dot(f2_ref[...], zz, preferred_element_type=F32)
        n2 = p.shape[0] // 2
        pr = p[:n2].astype(BF16)
        pim = p[n2:].astype(BF16)
        outs = []
        for grp in range(FFT_GROUPS):
            sl = slice(grp * FFT_GROUP, (grp + 1) * FFT_GROUP)
            lhs = jnp.concatenate([pr[:, sl], pim[:, sl]], axis=1)
            outs.append(jnp.dot(lhs, cs_ref[...], preferred_element_type=F32))
        o_ref[:, i, :] = (jnp.concatenate(outs, axis=1) * norm).astype(o_ref.dtype)


def _fourier_long(g):
    n = g.shape[0]
    n1 = FFT_N1
    assert n % (n1 * SUBLANES) == 0, n
    n2 = n // n1
    c1, s1 = _dft_tables(n1)
    f1 = jnp.asarray(np.concatenate([c1, -s1], axis=0), BF16)
    tj = min(n2, SUBLANES)
    z = pl.pallas_call(
        _fft_stage1_kernel,
        grid=(n2 // tj,),
        in_specs=[pl.BlockSpec((2 * n1, n1), lambda j: (0, 0)),
                  pl.BlockSpec((n1, tj * GROUP_WIDTH), lambda j: (0, j))],
        out_specs=pl.BlockSpec((2, n1, tj, GROUP_WIDTH), lambda j: (0, 0, j, 0)),
        out_shape=jax.ShapeDtypeStruct((2, n1, n2, GROUP_WIDTH), F32),
        compiler_params=_cparams("arbitrary"),
        name="fourier_stage1",
    )(f1, g.reshape(n1, n2 * GROUP_WIDTH))

    k1 = jnp.arange(n1, dtype=F32)[:, None]
    j2 = jnp.arange(n2, dtype=F32)[None, :]
    ang = (2.0 * np.pi / n) * (k1 * j2)
    twr = jnp.broadcast_to(jnp.cos(ang)[:, :, None], (n1, n2, LANES))
    twi = jnp.broadcast_to(-jnp.sin(ang)[:, :, None], (n1, n2, LANES))
    c2, s2 = _dft_tables(n2)
    f2 = jnp.asarray(np.block([[c2, s2], [-s2, c2]]), BF16)
    cc, sc = _dft_tables(FFT_GROUP)
    cs = jnp.asarray(np.concatenate([cc, sc], axis=0), BF16)
    tk1 = SUBLANES
    out = pl.pallas_call(
        functools.partial(_fft_stage2_kernel, norm=float((n * FFT_GROUP) ** -0.5)),
        grid=(n1 // tk1,),
        in_specs=[pl.BlockSpec((2, tk1, n2, GROUP_WIDTH), lambda i: (0, i, 0, 0)),
                  pl.BlockSpec((tk1, n2, LANES), lambda i: (i, 0, 0)),
                  pl.BlockSpec((tk1, n2, LANES), lambda i: (i, 0, 0)),
                  pl.BlockSpec((2 * n2, 2 * n2), lambda i: (0, 0)),
                  pl.BlockSpec((2 * FFT_GROUP, FFT_GROUP), lambda i: (0, 0))],
        out_specs=pl.BlockSpec((n2, tk1, GROUP_WIDTH), lambda i: (0, i, 0)),
        out_shape=jax.ShapeDtypeStruct((n2, n1, GROUP_WIDTH), BF16),
        compiler_params=_cparams("arbitrary"),
        name="fourier_stage2",
    )(z, twr, twi, f2, cs)
    return out.reshape(n, GROUP_WIDTH)


def _layernorm(z, g, b):
    mu = jnp.mean(z, axis=-1, keepdims=True)
    zc = z - mu
    var = jnp.mean(zc * zc, axis=-1, keepdims=True)
    return zc * lax.rsqrt(var + LN_EPS) * g + b


def _outproj_kernel(da_ref, lru_ref, mla_ref, fft_ref, x_ref, gate_ref, w_ref, g_ref, b_ref, o_ref, *, alpha):
    y = jnp.dot(da_ref[...], w_ref[0:GROUP_WIDTH], preferred_element_type=F32)
    y += jnp.dot(lru_ref[...], w_ref[GROUP_WIDTH:2 * GROUP_WIDTH], preferred_element_type=F32)
    y += jnp.dot(mla_ref[...], w_ref[2 * GROUP_WIDTH:3 * GROUP_WIDTH], preferred_element_type=F32)
    y += jnp.dot(fft_ref[...], w_ref[3 * GROUP_WIDTH:4 * GROUP_WIDTH], preferred_element_type=F32)
    z = alpha * x_ref[...] + gate_ref[...] * y
    o_ref[...] = _layernorm(z, g_ref[...], b_ref[...])


def _outproj_ln(parts, x, gate, w_out, layer, ln_g, ln_b, alpha, tm):
    n, d = x.shape
    tm = _row_tile(n, tm)
    row = lambda i: (i, 0)
    fixed = lambda i: (0, 0)
    return pl.pallas_call(
        functools.partial(_outproj_kernel, alpha=alpha),
        grid=(n // tm,),
        in_specs=[pl.BlockSpec((tm, GROUP_WIDTH), row)] * 4 + [
            pl.BlockSpec((tm, d), row),
            pl.BlockSpec((1, d), fixed),
            pl.BlockSpec((None,) + w_out.shape[1:], lambda i: (layer, 0, 0)),
            pl.BlockSpec((1, d), fixed),
            pl.BlockSpec((1, d), fixed),
        ],
        out_specs=pl.BlockSpec((tm, d), row),
        out_shape=jax.ShapeDtypeStruct((n, d), F32),
        compiler_params=_cparams("arbitrary"),
        name="out_proj_ln",
    )(*parts, x, gate, w_out, ln_g, ln_b)


def _ffn_kernel(x_ref, sh_ref, sc_ref, gate_ref, wg_ref, wu_ref, wd_ref, g_ref, b_ref, o_ref,
                h_ref, acc_ref, *, alpha):
    f = pl.program_id(1)

    @pl.when(f == 0)
    def _():
        h_ref[...] = (x_ref[...] * (1.0 + sc_ref[...]) + sh_ref[...]).astype(BF16)
        acc_ref[...] = jnp.zeros(acc_ref.shape, F32)

    h = h_ref[...]
    a = jnp.dot(h, wg_ref[...], preferred_element_type=F32)
    u = jnp.dot(h, wu_ref[...], preferred_element_type=F32)
    act = (a * jax.nn.sigmoid(a) * u).astype(BF16)
    acc_ref[...] += jnp.dot(act, wd_ref[...], preferred_element_type=F32)

    @pl.when(f == pl.num_programs(1) - 1)
    def _():
        z = alpha * x_ref[...] + gate_ref[...] * acc_ref[...]
        o_ref[...] = _layernorm(z, g_ref[...], b_ref[...])


def _ffn_ln(x, shift, scale, gate, wg, wu, wd, layer, ln_g, ln_b, alpha, tm, tf):
    n, d = x.shape
    d_ff = wg.shape[2]
    tm = _row_tile(n, tm)
    assert d_ff % tf == 0
    row = lambda i, f: (i, 0)
    fixed = lambda i, f: (0, 0)
    return pl.pallas_call(
        functools.partial(_ffn_kernel, alpha=alpha),
        grid=(n // tm, d_ff // tf),
        in_specs=[
            pl.BlockSpec((tm, d), row),
            pl.BlockSpec((1, d), fixed),
            pl.BlockSpec((1, d), fixed),
            pl.BlockSpec((1, d), fixed),
            pl.BlockSpec((None, d, tf), lambda i, f: (layer, 0, f)),
            pl.BlockSpec((None, d, tf), lambda i, f: (layer, 0, f)),
            pl.BlockSpec((None, tf, d), lambda i, f: (layer, f, 0)),
            pl.BlockSpec((1, d), fixed),
            pl.BlockSpec((1, d), fixed),
        ],
        out_specs=pl.BlockSpec((tm, d), row),
        out_shape=jax.ShapeDtypeStruct((n, d), F32),
        scratch_shapes=[pltpu.VMEM((tm, d), BF16), pltpu.VMEM((tm, d), F32)],
        compiler_params=_cparams("arbitrary", "arbitrary"),
        name="ffn_ln",
    )(x, shift, scale, gate, wg, wu, wd, ln_g, ln_b)


def _pad_in_weight(w_in):
    src_fft = 3 * GROUP_WIDTH + 2 * GROUP_WIDTH + MLA_Q_RANK + MLA_KV_RANK + MLA_ROPE
    pad = jnp.zeros(w_in.shape[:2] + (SEG_FFT - (SEG_MLA + MLA_Q_RANK + MLA_KV_RANK + MLA_ROPE),), BF16)
    w = w_in.astype(BF16)
    return jnp.concatenate([w[..., :src_fft], pad, w[..., src_fft:]], axis=-1)


def _pad_mla_weights(w_uq, w_ukv):
    qr = w_uq.shape[0]
    wq = w_uq.reshape(qr, HEADS, MLA_NOPE + MLA_ROPE)
    wq = jnp.concatenate([wq, jnp.zeros((qr, HEADS, MLA_HEAD_PAD - MLA_NOPE - MLA_ROPE), w_uq.dtype)], axis=2)
    wkv = w_ukv.reshape(w_ukv.shape[0], HEADS, MLA_NOPE + MLA_V)
    wk = wkv[:, :, :MLA_NOPE].reshape(w_ukv.shape[0], HEADS * MLA_NOPE)
    wv = wkv[:, :, MLA_NOPE:].reshape(w_ukv.shape[0], HEADS * MLA_V)
    return (wq.reshape(qr, HEADS * MLA_HEAD_PAD).astype(BF16), wk.astype(BF16), wv.astype(BF16))


def _rope_tables(n):
    rows = n // GRID_W
    row = jnp.repeat(jnp.arange(rows, dtype=F32), GRID_W)
    col = jnp.tile(jnp.arange(GRID_W, dtype=F32), rows)
    axis_dim = ROT_DIM // 2
    inv = ROPE_BASE ** (-jnp.arange(0, axis_dim, 2, dtype=F32) / axis_dim)
    ang = jnp.concatenate([row[:, None] * inv, col[:, None] * inv], axis=-1)
    cos, sin = jnp.cos(ang), jnp.sin(ang)
    cos_slab = jnp.concatenate([cos, cos] * (LANES // ROT_DIM), axis=1)
    sin_slab = jnp.concatenate([-sin, sin] * (LANES // ROT_DIM), axis=1)
    return cos_slab, sin_slab


def kernel(x, c, ctx, c_ctx, w_ada, b_ada, w_in, w_out, ln1_g, ln1_b, ln2_g, ln2_b,
           da_lq1, da_lk1, da_lq2, da_lk2, da_subln_g,
           lru_conv_w, lru_conv_b, lru_wr, lru_br, lru_wi, lru_bi, lru_lam,
           mla_qn_g, mla_wuq, mla_kvn_g, mla_wukv,
           ffn_wg, ffn_wu, ffn_wd):
    assert x.shape[0] == 1 and c.shape[0] == 1 and ctx.shape[0] == 1
    depth = w_ada.shape[0]
    n, d = x.shape[1], x.shape[2]
    nc = ctx.shape[1]
    alpha = (2 * depth) ** 0.25
    x_lat, x_ctx = x[0], ctx[0]

    cos_l, sin_l = _rope_tables(n)
    cos_c, sin_c = jnp.ones((nc, LANES), F32), jnp.zeros((nc, LANES), F32)

    c8 = jnp.zeros((SUBLANES, d), F32).at[0].set(c[0]).at[1].set(c_ctx)
    mods = _ada_mods(c8, w_ada, b_ada)

    w_in_p = _pad_in_weight(w_in)
    w_out_b = w_out.astype(BF16)
    wg_b, wu_b, wd_b = ffn_wg.astype(BF16), ffn_wu.astype(BF16), ffn_wd.astype(BF16)

    row2 = lambda v: v.reshape(1, -1)
    for l in range(depth):
        need_ctx = l < depth - 1
        lambda_init = 0.8 - 0.6 * math.exp(-0.3 * l)
        m_lat = [mods[l, 0:1, i * d:(i + 1) * d] for i in range(6)]
        m_ctx = [mods[l, 1:2, i * d:(i + 1) * d] for i in range(6)]

        wq_p, wk_p, wv_p = _pad_mla_weights(mla_wuq[l], mla_wukv[l])

        ql, kl, vl, lru_l, mla_l, fft_l = _inproj(x_lat, m_lat[0], m_lat[1], cos_l, sin_l, w_in_p, l, 512)
        qc, kc, vc, lru_c, mla_c, fft_c = _inproj(x_ctx, m_ctx[0], m_ctx[1], cos_c, sin_c, w_in_p, l, 256)

        lams = [row2(da_lq1[l]), row2(da_lk1[l]), row2(da_lq2[l]), row2(da_lk2[l])]
        g_da = row2(da_subln_g[l])
        da_l = _diff_attention(ql, kl, vl, (kc, vc), lams, g_da, lambda_init, ATTN_TQ, ATTN_TK, ATTN_CHUNK)

        lru_args = lambda dr: (lru_conv_w[l], row2(lru_conv_b[l]), lru_wr[l, dr].astype(BF16), row2(lru_br[l, dr]),
                               lru_wi[l, dr].astype(BF16), row2(lru_bi[l, dr]), row2(lru_lam[l, dr]))
        h0 = jnp.zeros((1, GROUP_WIDTH), F32)
        hc_f, s_f = _lru_pass(lru_c, *lru_args(0), h0, None, False, 256)
        lru_yc, s_b = _lru_pass(lru_c, *lru_args(1), h0, hc_f, True, 256)
        hl_f, _ = _lru_pass(lru_l, *lru_args(0), s_f, None, False, 256)
        lru_yl, _ = _lru_pass(lru_l, *lru_args(1), s_b, hl_f, True, 256)

        mq_l, mk_l, mv_l = _mla_prep(mla_l, cos_l, sin_l, row2(mla_qn_g[l]), row2(mla_kvn_g[l]), wq_p, wk_p, wv_p, 512)
        mq_c, mk_c, mv_c = _mla_prep(mla_c, cos_c, sin_c, row2(mla_qn_g[l]), row2(mla_kvn_g[l]), wq_p, wk_p, wv_p, 256)
        mla_yl = _mla_attention(mq_l, mk_l, mv_l, (mk_c, mv_c), 2 * ATTN_TQ, ATTN_TK, ATTN_CHUNK)

        fft_yl = _fourier_long(fft_l)

        x_lat = _outproj_ln([da_l, lru_yl, mla_yl, fft_yl], x_lat, m_lat[2], w_out_b, l,
                            row2(ln1_g[l]), row2(ln1_b[l]), alpha, 512)
        x_lat = _ffn_ln(x_lat, m_lat[3], m_lat[4], m_lat[5], wg_b, wu_b, wd_b, l,
                        row2(ln2_g[l]), row2(ln2_b[l]), alpha, 512, 512)

        if need_ctx:
            da_c = _diff_attention(qc, kc, vc, None, lams, g_da, lambda_init, 256, 256, 256)
            mla_yc = _mla_attention(mq_c, mk_c, mv_c, None, 256, 256, 256)
            fft_yc = _fourier_small(fft_c)
            x_ctx = _outproj_ln([da_c, lru_yc, mla_yc, fft_yc], x_ctx, m_ctx[2], w_out_b, l,
                                row2(ln1_g[l]), row2(ln1_b[l]), alpha, 256)
            x_ctx = _ffn_ln(x_ctx, m_ctx[3], m_ctx[4], m_ctx[5], wg_b, wu_b, wd_b, l,
                            row2(ln2_g[l]), row2(ln2_b[l]), alpha, 256, 512)
    return x_lat[None]
```

```python
import functools
import math

import numpy as np
import jax
import jax.numpy as jnp
from jax import lax
from jax.experimental import pallas as pl
from jax.experimental.pallas import tpu as pltpu

F32 = jnp.float32
BF16 = jnp.bfloat16

D_MODEL = 2048
GRID_W = 64
ROPE_BASE = 10000.0
LN_EPS = 1e-5
RMS_EPS = 1e-6

HEADS = 4
DA_HEAD_DIM = 64
GROUP_WIDTH = 512
LRU_BLOCKS = 4
LRU_BLOCK = GROUP_WIDTH // LRU_BLOCKS
CONV_W = 4
CONV_LEFT = 2
LRU_C = 8.0
MLA_Q_RANK = 384
MLA_KV_RANK = 256
MLA_NOPE = 128
MLA_ROPE = 64
MLA_V = 128
FFT_GROUPS = 4
FFT_GROUP = 128
ROT_DIM = 64
ROT_HALF = ROT_DIM // 2

LANES = 128
SUBLANES = 8
SEG_Q = 0
SEG_K = 512
SEG_V = 1024
SEG_LRU = 1536
SEG_MLA = 2560
SEG_FFT = 3328
IN_PAD = 3840
MLA_SEG = 768
MLA_HEAD_PAD = 256

FFT_N1 = 128

VMEM_LIMIT = 56 * 1024 * 1024
LOG2E = math.log2(math.e)

ATTN_TQ = 1024
ATTN_TK = 4096
ATTN_CHUNK = 512
ONES_ROWS = 16
MAX_EXCESS = 32.0


def _cparams(*sem):
    return pltpu.CompilerParams(dimension_semantics=sem, vmem_limit_bytes=VMEM_LIMIT)


def _row_tile(n, want):
    t = min(n, want)
    assert n % t == 0, (n, t)
    return t


def _ada_kernel(c_ref, w_ref, b_ref, o_ref):
    s = c_ref[...]
    s = s * jax.nn.sigmoid(s)
    o_ref[...] = jnp.dot(s.astype(BF16), w_ref[...].astype(BF16), preferred_element_type=F32) + b_ref[...]


def _ada_mods(c8, w_ada, b_ada):
    depth, d, n6 = w_ada.shape
    tn = 1536
    return pl.pallas_call(
        _ada_kernel,
        grid=(depth, n6 // tn),
        in_specs=[
            pl.BlockSpec((SUBLANES, d), lambda l, j: (0, 0)),
            pl.BlockSpec((None, d, tn), lambda l, j: (l, 0, j)),
            pl.BlockSpec((None, 1, tn), lambda l, j: (l, 0, j)),
        ],
        out_specs=pl.BlockSpec((None, SUBLANES, tn), lambda l, j: (l, 0, j)),
        out_shape=jax.ShapeDtypeStruct((depth, SUBLANES, n6), F32),
        compiler_params=_cparams("arbitrary", "arbitrary"),
        name="ada_mods",
    )(c8, w_ada, b_ada.reshape(depth, 1, n6))


def _rope_slab(x, cos, sin_signed):
    lane = lax.broadcasted_iota(jnp.int32, x.shape, 1)
    partner = jnp.where((lane & ROT_HALF) == 0,
                        pltpu.roll(x, LANES - ROT_HALF, 1),
                        pltpu.roll(x, ROT_HALF, 1))
    return x * cos + partner * sin_signed


def _inproj_kernel(x_ref, sh_ref, sc_ref, cos_ref, sin_ref, w_ref,
                   q_ref, k_ref, v_ref, lru_ref, mla_ref, fft_ref, *, q_scale):
    xm = (x_ref[...] * (1.0 + sc_ref[...]) + sh_ref[...]).astype(BF16)
    cos = cos_ref[...]
    sin = sin_ref[...]

    def seg(a, width):
        return jnp.dot(xm, w_ref[:, a:a + width], preferred_element_type=F32)

    q_all = seg(SEG_Q, GROUP_WIDTH)
    k_all = seg(SEG_K, GROUP_WIDTH)
    for h in range(HEADS):
        o = h * LANES
        q_ref[:, o:o + LANES] = (_rope_slab(q_all[:, o:o + LANES], cos, sin) * q_scale).astype(BF16)
        k_ref[:, o:o + LANES] = _rope_slab(k_all[:, o:o + LANES], cos, sin).astype(BF16)
    v_ref[...] = seg(SEG_V, GROUP_WIDTH).T.astype(BF16)
    lru_ref[...] = seg(SEG_LRU, 2 * GROUP_WIDTH)
    mla_ref[...] = seg(SEG_MLA, MLA_SEG)
    fft_ref[...] = seg(SEG_FFT, GROUP_WIDTH).astype(BF16)


def _inproj(x, shift, scale, cos, sin, w_in_p, layer, tm):
    n, d = x.shape
    tm = _row_tile(n, tm)
    row = lambda i: (i, 0)
    col = lambda i: (0, i)
    fixed = lambda i: (0, 0)
    outs = [
        jax.ShapeDtypeStruct((n, GROUP_WIDTH), BF16),
        jax.ShapeDtypeStruct((n, GROUP_WIDTH), BF16),
        jax.ShapeDtypeStruct((GROUP_WIDTH, n), BF16),
        jax.ShapeDtypeStruct((n, 2 * GROUP_WIDTH), F32),
        jax.ShapeDtypeStruct((n, MLA_SEG), F32),
        jax.ShapeDtypeStruct((n, GROUP_WIDTH), BF16),
    ]
    out_specs = [pl.BlockSpec((tm, o.shape[1]), row) for o in outs]
    out_specs[2] = pl.BlockSpec((GROUP_WIDTH, tm), col)
    return pl.pallas_call(
        functools.partial(_inproj_kernel, q_scale=DA_HEAD_DIM ** -0.5 * LOG2E),
        grid=(n // tm,),
        in_specs=[
            pl.BlockSpec((tm, d), row),
            pl.BlockSpec((1, d), fixed),
            pl.BlockSpec((1, d), fixed),
            pl.BlockSpec((tm, LANES), row),
            pl.BlockSpec((tm, LANES), row),
            pl.BlockSpec((None, d, IN_PAD), lambda i: (layer, 0, 0)),
        ],
        out_specs=out_specs,
        out_shape=outs,
        compiler_params=_cparams("arbitrary"),
        name="in_proj",
    )(x, shift, scale, cos, sin, w_in_p)


def _scores_t(q_t, k):
    return jnp.dot(k, q_t, preferred_element_type=F32)


def _probs_t(s_t, ref_max):
    return jnp.exp2(s_t - ref_max).astype(BF16)


def _weighted_values_t(v_t, p_t):
    v_ext = jnp.concatenate([v_t, jnp.ones((ONES_ROWS, v_t.shape[1]), BF16)], axis=0)
    return jnp.dot(v_ext, p_t, preferred_element_type=F32)


def _softmax_update(s_t, v_t, m_ref, acc_ref, c):
    m_prev = m_ref[c]
    m_new = jnp.maximum(m_prev, jnp.max(s_t, axis=0, keepdims=True))
    alpha = jnp.exp2(m_prev - m_new)
    acc_ref[c] = alpha * acc_ref[c] + _weighted_values_t(v_t, _probs_t(s_t, m_new))
    m_ref[c] = m_new


def _flash_kernel(*refs, n_comp, has_prefix, chunk, epilogue):
    refs = list(refs)
    q_ref = refs.pop(0)
    if has_prefix:
        kc_ref, vtc_ref = refs.pop(0), refs.pop(0)
    k_ref, vt_ref = refs.pop(0), refs.pop(0)
    qt_ref, m_ref, acc_ref = refs[-3:]
    o_ref = refs[-4]
    extra = refs[:-4]
    j = pl.program_id(2)

    @pl.when(j == 0)
    def _():
        m_ref[...] = jnp.full(m_ref.shape, -jnp.inf, F32)
        acc_ref[...] = jnp.zeros(acc_ref.shape, F32)
        q_t = q_ref[...].astype(F32).T
        if n_comp == 2:
            row = lax.broadcasted_iota(jnp.int32, q_t.shape, 0)
            qt_ref[0] = jnp.where(row < DA_HEAD_DIM, q_t, 0.0).astype(BF16)
            qt_ref[1] = jnp.where(row >= DA_HEAD_DIM, q_t, 0.0).astype(BF16)
        else:
            qt_ref[0] = q_t.astype(BF16)
        if has_prefix:
            for c in range(n_comp):
                _softmax_update(_scores_t(qt_ref[c], kc_ref[...]), vtc_ref[...], m_ref, acc_ref, c)

    n_chunks = k_ref.shape[0] // chunk

    def exact_step():
        def body(i, carry):
            off = pl.multiple_of(i * chunk, chunk)
            for c in range(n_comp):
                _softmax_update(_scores_t(qt_ref[c], k_ref[pl.ds(off, chunk), :]), vt_ref[:, pl.ds(off, chunk)],
                                m_ref, acc_ref, c)
            return carry
        lax.fori_loop(0, n_chunks, body, 0)

    if not has_prefix:
        exact_step()
    else:
        ref_max = [m_ref[c] for c in range(n_comp)]
        step_acc = [None] * n_comp
        step_max = [None] * n_comp
        for i in range(n_chunks):
            off = i * chunk
            for c in range(n_comp):
                s_t = _scores_t(qt_ref[c], k_ref[off:off + chunk, :])
                unit_max = jnp.max(s_t, axis=0, keepdims=True)
                pv = _weighted_values_t(vt_ref[:, off:off + chunk], _probs_t(s_t, ref_max[c]))
                step_acc[c] = pv if step_acc[c] is None else step_acc[c] + pv
                step_max[c] = unit_max if step_max[c] is None else jnp.maximum(step_max[c], unit_max)
        excess = jnp.max(step_max[0] - ref_max[0])
        for c in range(1, n_comp):
            excess = jnp.maximum(excess, jnp.max(step_max[c] - ref_max[c]))
        in_range = excess <= MAX_EXCESS

        @pl.when(in_range)
        def _():
            for c in range(n_comp):
                m_new = jnp.maximum(ref_max[c], step_max[c])
                acc_ref[c] = (acc_ref[c] + step_acc[c]) * jnp.exp2(ref_max[c] - m_new)
                m_ref[c] = m_new

        @pl.when(jnp.logical_not(in_range))
        def _():
            exact_step()

    @pl.when(j == pl.num_programs(2) - 1)
    def _():
        epilogue(extra, o_ref, acc_ref)


def _normalised_t(acc_ref, c):
    dv = acc_ref.shape[1] - ONES_ROWS
    return acc_ref[c, 0:dv, :] / acc_ref[c, dv:dv + 1, :]


def _da_epilogue(extra, o_ref, acc_ref, *, lambda_init):
    lq1_ref, lk1_ref, lq2_ref, lk2_ref, g_ref = extra
    lam = (jnp.exp(jnp.sum(lq1_ref[...] * lk1_ref[...], keepdims=True))
           - jnp.exp(jnp.sum(lq2_ref[...] * lk2_ref[...], keepdims=True)) + lambda_init)
    o = (_normalised_t(acc_ref, 0) - lam * _normalised_t(acc_ref, 1)).T
    inv = lax.rsqrt(jnp.mean(o * o, axis=-1, keepdims=True) + RMS_EPS)
    o_ref[...] = ((o * inv * g_ref[...]) * (1.0 - lambda_init)).astype(o_ref.dtype)


def _mla_epilogue(extra, o_ref, acc_ref):
    o_ref[...] = _normalised_t(acc_ref, 0).T.astype(o_ref.dtype)


def _flash_attention(q, k, v_t, prefix, extra, extra_specs, *, dk, n_comp, epilogue, tq, tk, chunk, name):
    nq, nk = q.shape[0], k.shape[0]
    dv = v_t.shape[0] // HEADS
    tq, tk = _row_tile(nq, tq), _row_tile(nk, tk)
    chunk = _row_tile(tk, chunk)
    has_prefix = prefix is not None
    qspec = pl.BlockSpec((tq, dk), lambda h, i, j: (i, h))
    in_specs, args = [qspec], [q]
    if has_prefix:
        nc = prefix[0].shape[0]
        in_specs += [pl.BlockSpec((nc, dk), lambda h, i, j: (0, h)),
                     pl.BlockSpec((dv, nc), lambda h, i, j: (h, 0))]
        args += list(prefix)
    in_specs += [pl.BlockSpec((tk, dk), lambda h, i, j: (j, h)),
                 pl.BlockSpec((dv, tk), lambda h, i, j: (h, j))] + list(extra_specs)
    args += [k, v_t] + list(extra)
    return pl.pallas_call(
        functools.partial(_flash_kernel, n_comp=n_comp, has_prefix=has_prefix, chunk=chunk, epilogue=epilogue),
        grid=(HEADS, nq // tq, nk // tk),
        in_specs=in_specs,
        out_specs=pl.BlockSpec((tq, dv), lambda h, i, j: (i, h)),
        out_shape=jax.ShapeDtypeStruct((nq, HEADS * dv), BF16),
        scratch_shapes=[
            pltpu.VMEM((n_comp, dk, tq), BF16),
            pltpu.VMEM((n_comp, 1, tq), F32),
            pltpu.VMEM((n_comp, dv + ONES_ROWS, tq), F32),
        ],
        compiler_params=_cparams("arbitrary", "arbitrary", "arbitrary"),
        name=name,
    )(*args)


def _diff_attention(q, k, v_t, prefix, lams, g, lambda_init, tq, tk, chunk):
    small = lambda w: pl.BlockSpec((1, w), lambda h, i, j: (0, 0))
    return _flash_attention(
        q, k, v_t, prefix, list(lams) + [g], [small(DA_HEAD_DIM)] * 4 + [small(LANES)],
        dk=2 * DA_HEAD_DIM, n_comp=2, epilogue=functools.partial(_da_epilogue, lambda_init=lambda_init),
        tq=tq, tk=tk, chunk=chunk, name="diff_attention")


def _mla_attention(q, k, v_t, prefix, tq, tk, chunk):
    return _flash_attention(q, k, v_t, prefix, [], [], dk=MLA_HEAD_PAD, n_comp=1, epilogue=_mla_epilogue,
                            tq=tq, tk=tk, chunk=chunk, name="mla_attention")


def _rms(x, g):
    inv = lax.rsqrt(jnp.mean(x * x, axis=-1, keepdims=True) + RMS_EPS)
    return x * inv * g


def _mla_prep_kernel(u_ref, cos_ref, sin_ref, qg_ref, kvg_ref, wq_ref, wk_ref, wv_ref,
                     q_ref, k_ref, v_ref, *, scale):
    cos = cos_ref[...]
    sin = sin_ref[...]
    cq = _rms(u_ref[:, 0:MLA_Q_RANK], qg_ref[...]).astype(BF16)
    ckv = _rms(u_ref[:, MLA_Q_RANK:MLA_Q_RANK + MLA_KV_RANK], kvg_ref[...]).astype(BF16)
    k_rope = _rope_slab(u_ref[:, MLA_Q_RANK + MLA_KV_RANK:MLA_SEG], cos, sin).astype(BF16)
    k_nope = jnp.dot(ckv, wk_ref[...], preferred_element_type=F32)
    for h in range(HEADS):
        o = h * MLA_HEAD_PAD
        q_h = jnp.dot(cq, wq_ref[:, o:o + MLA_HEAD_PAD], preferred_element_type=F32)
        q_ref[:, o:o + LANES] = (q_h[:, :LANES] * scale).astype(BF16)
        q_ref[:, o + LANES:o + 2 * LANES] = (_rope_slab(q_h[:, LANES:], cos, sin) * scale).astype(BF16)
        k_ref[:, o:o + LANES] = k_nope[:, h * LANES:(h + 1) * LANES].astype(BF16)
        k_ref[:, o + LANES:o + 2 * LANES] = k_rope
    v_ref[...] = jnp.dot(ckv, wv_ref[...], preferred_element_type=F32).T.astype(BF16)


def _mla_prep(u_mla, cos, sin, qn_g, kvn_g, wq_p, wk_p, wv_p, tm):
    n = u_mla.shape[0]
    tm = _row_tile(n, tm)
    row = lambda i: (i, 0)
    fixed = lambda i: (0, 0)
    outs = [
        jax.ShapeDtypeStruct((n, HEADS * MLA_HEAD_PAD), BF16),
        jax.ShapeDtypeStruct((n, HEADS * MLA_HEAD_PAD), BF16),
        jax.ShapeDtypeStruct((GROUP_WIDTH, n), BF16),
    ]
    out_specs = [pl.BlockSpec((tm, HEADS * MLA_HEAD_PAD), row)] * 2 + [pl.BlockSpec((GROUP_WIDTH, tm), lambda i: (0, i))]
    return pl.pallas_call(
        functools.partial(_mla_prep_kernel, scale=(MLA_NOPE + MLA_ROPE) ** -0.5 * LOG2E),
        grid=(n // tm,),
        in_specs=[
            pl.BlockSpec((tm, MLA_SEG), row),
            pl.BlockSpec((tm, LANES), row),
            pl.BlockSpec((tm, LANES), row),
            pl.BlockSpec((1, MLA_Q_RANK), fixed),
            pl.BlockSpec((1, MLA_KV_RANK), fixed),
            pl.BlockSpec(wq_p.shape, fixed),
            pl.BlockSpec(wk_p.shape, fixed),
            pl.BlockSpec(wv_p.shape, fixed),
        ],
        out_specs=out_specs,
        out_shape=outs,
        compiler_params=_cparams("arbitrary"),
        name="mla_prep",
    )(u_mla, cos, sin, qn_g, kvn_g, wq_p, wk_p, wv_p)


def _shift_rows(x, d, halo):
    tt = x.shape[0]
    row8 = lax.broadcasted_iota(jnp.int32, (SUBLANES, x.shape[1]), 0)
    if d > 0:
        r = pltpu.roll(x, d, 0)
        f = pltpu.roll(halo, d, 0)
        first = jnp.where(row8 < d, f, r[:SUBLANES])
        return jnp.concatenate([first, r[SUBLANES:]], axis=0) if tt > SUBLANES else first
    r = pltpu.roll(x, tt + d, 0)
    f = pltpu.roll(halo, SUBLANES + d, 0)
    last = jnp.where(row8 >= SUBLANES + d, f, r[tt - SUBLANES:])
    return jnp.concatenate([r[:tt - SUBLANES], last], axis=0) if tt > SUBLANES else last


def _lru_kernel(*refs, reverse, final):
    if final:
        (x_ref, prev_ref, next_ref, cw_ref, cb_ref, wr_ref, br_ref, wi_ref, bi_ref, lam_ref, h0_ref,
         gate_ref, hf_ref, y_ref, hlast_ref, carry_ref) = refs
    else:
        (x_ref, prev_ref, next_ref, cw_ref, cb_ref, wr_ref, br_ref, wi_ref, bi_ref, lam_ref, h0_ref,
         y_ref, hlast_ref, carry_ref) = refs
    i = pl.program_id(0)
    nt = pl.num_programs(0)
    t = (nt - 1 - i) if reverse else i
    x = x_ref[...]
    tt = x.shape[0]

    @pl.when(i == 0)
    def _():
        carry_ref[...] = h0_ref[...]

    prev = prev_ref[...] * (t > 0).astype(F32)
    nxt = next_ref[...] * (t < nt - 1).astype(F32)
    cw = cw_ref[...]
    xc = (cw[0:1] * _shift_rows(x, 2, prev) + cw[1:2] * _shift_rows(x, 1, prev)
          + cw[2:3] * x + cw[3:4] * _shift_rows(x, -1, nxt)) + cb_ref[...]

    xb = xc.astype(BF16)

    def gate(w_ref, b_ref):
        z = jnp.concatenate(
            [jnp.dot(xb[:, b * LRU_BLOCK:(b + 1) * LRU_BLOCK], w_ref[b], preferred_element_type=F32)
             for b in range(LRU_BLOCKS)], axis=1)
        return jax.nn.sigmoid(z + b_ref[...])

    r = gate(wr_ref, br_ref)
    ig = gate(wi_ref, bi_ref)
    log_a = -LRU_C * r * jax.nn.softplus(-lam_ref[...])
    a = jnp.exp(log_a)
    u = jnp.sqrt(1.0 - jnp.exp(2.0 * log_a)) * (ig * xc)

    row = lax.broadcasted_iota(jnp.int32, a.shape, 0)
    big_a, big_b = a, u
    d = 1
    while d < tt:
        if reverse:
            valid = row < tt - d
            a_s = pltpu.roll(big_a, tt - d, 0)
            b_s = pltpu.roll(big_b, tt - d, 0)
        else:
            valid = row >= d
            a_s = pltpu.roll(big_a, d, 0)
            b_s = pltpu.roll(big_b, d, 0)
        big_b = jnp.where(valid, big_a * b_s + big_b, big_b)
        big_a = jnp.where(valid, big_a * a_s, big_a)
        d *= 2
    h = big_a * carry_ref[...] + big_b
    edge = h[0:1] if reverse else h[tt - 1:tt]
    carry_ref[...] = edge
    hlast_ref[...] = edge
    if final:
        y_ref[...] = ((hf_ref[...] + h) * jax.nn.gelu(gate_ref[...])).astype(y_ref.dtype)
    else:
        y_ref[...] = h


def _lru_pass(u_lru, conv_w, conv_b, wr, br, wi, bi, lam, h0, h_fwd, reverse, tt):
    n = u_lru.shape[0]
    tt = _row_tile(n, tt)
    nt = n // tt
    per8 = tt // SUBLANES
    final = h_fwd is not None
    pos = (lambda i: nt - 1 - i) if reverse else (lambda i: i)
    tile = lambda i: (pos(i), 0)
    fixed = lambda i: (0, 0)
    fixed3 = lambda i: (0, 0, 0)
    in_specs = [
        pl.BlockSpec((tt, GROUP_WIDTH), tile),
        pl.BlockSpec((SUBLANES, GROUP_WIDTH), lambda i: (jnp.maximum(pos(i) * per8 - 1, 0), 0)),
        pl.BlockSpec((SUBLANES, GROUP_WIDTH), lambda i: (jnp.minimum((pos(i) + 1) * per8, n // SUBLANES - 1), 0)),
        pl.BlockSpec((CONV_W, GROUP_WIDTH), fixed),
        pl.BlockSpec((1, GROUP_WIDTH), fixed),
        pl.BlockSpec((LRU_BLOCKS, LRU_BLOCK, LRU_BLOCK), fixed3),
        pl.BlockSpec((1, GROUP_WIDTH), fixed),
        pl.BlockSpec((LRU_BLOCKS, LRU_BLOCK, LRU_BLOCK), fixed3),
        pl.BlockSpec((1, GROUP_WIDTH), fixed),
        pl.BlockSpec((1, GROUP_WIDTH), fixed),
        pl.BlockSpec((1, GROUP_WIDTH), fixed),
    ]
    args = [u_lru, u_lru, u_lru, conv_w, conv_b, wr, br, wi, bi, lam, h0]
    if final:
        in_specs += [pl.BlockSpec((tt, GROUP_WIDTH), lambda i: (pos(i), 1)),
                     pl.BlockSpec((tt, GROUP_WIDTH), tile)]
        args += [u_lru, h_fwd]
    return pl.pallas_call(
        functools.partial(_lru_kernel, reverse=reverse, final=final),
        grid=(nt,),
        in_specs=in_specs,
        out_specs=[pl.BlockSpec((tt, GROUP_WIDTH), tile), pl.BlockSpec((1, GROUP_WIDTH), fixed)],
        out_shape=[jax.ShapeDtypeStruct((n, GROUP_WIDTH), BF16 if final else F32),
                   jax.ShapeDtypeStruct((1, GROUP_WIDTH), F32)],
        scratch_shapes=[pltpu.VMEM((1, GROUP_WIDTH), F32)],
        compiler_params=_cparams("arbitrary"),
        name="rglru_bwd" if reverse else "rglru_fwd",
    )(*args)


def _dft_tables(n):
    k = np.arange(n, dtype=np.int64)
    ang = 2.0 * np.pi * ((k[:, None] * k[None, :]) % n).astype(np.float64) / n
    return np.cos(ang), np.sin(ang)


def _fft_small_kernel(g_ref, cs_ref, cn_ref, sn_ref, o_ref, *, norm):
    g = g_ref[...]
    a_parts, b_parts = [], []
    for grp in range(FFT_GROUPS):
        gg = g[:, grp * FFT_GROUP:(grp + 1) * FFT_GROUP]
        ab = jnp.dot(gg, cs_ref[...], preferred_element_type=F32)
        a_parts.append(ab[:, :FFT_GROUP])
        b_parts.append(ab[:, FFT_GROUP:])
    a = jnp.concatenate(a_parts, axis=1).astype(BF16)
    b = jnp.concatenate(b_parts, axis=1).astype(BF16)
    y = (jnp.dot(cn_ref[...], a, preferred_element_type=F32)
         - jnp.dot(sn_ref[...], b, preferred_element_type=F32))
    o_ref[...] = (y * norm).astype(o_ref.dtype)


def _fourier_small(g):
    n = g.shape[0]
    cc, sc = _dft_tables(FFT_GROUP)
    cn, sn = _dft_tables(n)
    cs = jnp.asarray(np.concatenate([cc, sc], axis=1), BF16)
    return pl.pallas_call(
        functools.partial(_fft_small_kernel, norm=float((n * FFT_GROUP) ** -0.5)),
        out_shape=jax.ShapeDtypeStruct((n, GROUP_WIDTH), BF16),
        compiler_params=pltpu.CompilerParams(vmem_limit_bytes=VMEM_LIMIT),
        name="fourier_ctx",
    )(g, cs, jnp.asarray(cn, BF16), jnp.asarray(sn, BF16))


def _fft_stage1_kernel(f_ref, x_ref, z_ref):
    res = jnp.dot(f_ref[...], x_ref[...], preferred_element_type=F32)
    tj = z_ref.shape[2]
    res = res.reshape(2, FFT_N1, tj * GROUP_WIDTH)
    for jj in range(tj):
        z_ref[:, :, jj, :] = res[:, :, jj * GROUP_WIDTH:(jj + 1) * GROUP_WIDTH]


def _fft_stage2_kernel(z_ref, twr_ref, twi_ref, f2_ref, cs_ref, o_ref, *, norm):
    tk1 = z_ref.shape[1]
    for i in range(tk1):
        zr = z_ref[0, i]
        zi = z_ref[1, i]
        twr = jnp.concatenate([twr_ref[i]] * (GROUP_WIDTH // LANES), axis=1)
        twi = jnp.concatenate([twi_ref[i]] * (GROUP_WIDTH // LANES), axis=1)
        zz = jnp.concatenate([zr * twr - zi * twi, zr * twi + zi * twr], axis=0).astype(BF16)
        p = jnp.dot(f2_ref[...], zz, preferred_element_type=F32)
        n2 = p.shape[0] // 2
        pr = p[:n2].astype(BF16)
        pim = p[n2:].astype(BF16)
        outs = []
        for grp in range(FFT_GROUPS):
            sl = slice(grp * FFT_GROUP, (grp + 1) * FFT_GROUP)
            lhs = jnp.concatenate([pr[:, sl], pim[:, sl]], axis=1)
            outs.append(jnp.dot(lhs, cs_ref[...], preferred_element_type=F32))
        o_ref[:, i, :] = (jnp.concatenate(outs, axis=1) * norm).astype(o_ref.dtype)


def _fourier_long(g):
    n = g.shape[0]
    n1 = FFT_N1
    assert n % (n1 * SUBLANES) == 0, n
    n2 = n // n1
    c1, s1 = _dft_tables(n1)
    f1 = jnp.asarray(np.concatenate([c1, -s1], axis=0), BF16)
    tj = min(n2, SUBLANES)
    z = pl.pallas_call(
        _fft_stage1_kernel,
        grid=(n2 // tj,),
        in_specs=[pl.BlockSpec((2 * n1, n1), lambda j: (0, 0)),
                  pl.BlockSpec((n1, tj * GROUP_WIDTH), lambda j: (0, j))],
        out_specs=pl.BlockSpec((2, n1, tj, GROUP_WIDTH), lambda j: (0, 0, j, 0)),
        out_shape=jax.ShapeDtypeStruct((2, n1, n2, GROUP_WIDTH), F32),
        compiler_params=_cparams("arbitrary"),
        name="fourier_stage1",
    )(f1, g.reshape(n1, n2 * GROUP_WIDTH))

    k1 = jnp.arange(n1, dtype=F32)[:, None]
    j2 = jnp.arange(n2, dtype=F32)[None, :]
    ang = (2.0 * np.pi / n) * (k1 * j2)
    twr = jnp.broadcast_to(jnp.cos(ang)[:, :, None], (n1, n2, LANES))
    twi = jnp.broadcast_to(-jnp.sin(ang)[:, :, None], (n1, n2, LANES))
    c2, s2 = _dft_tables(n2)
    f2 = jnp.asarray(np.block([[c2, s2], [-s2, c2]]), BF16)
    cc, sc = _dft_tables(FFT_GROUP)
    cs = jnp.asarray(np.concatenate([cc, sc], axis=0), BF16)
    tk1 = SUBLANES
    out = pl.pallas_call(
        functools.partial(_fft_stage2_kernel, norm=float((n * FFT_GROUP) ** -0.5)),
        grid=(n1 // tk1,),
        in_specs=[pl.BlockSpec((2, tk1, n2, GROUP_WIDTH), lambda i: (0, i, 0, 0)),
                  pl.BlockSpec((tk1, n2, LANES), lambda i: (i, 0, 0)),
                  pl.BlockSpec((tk1, n2, LANES), lambda i: (i, 0, 0)),
                  pl.BlockSpec((2 * n2, 2 * n2), lambda i: (0, 0)),
                  pl.BlockSpec((2 * FFT_GROUP, FFT_GROUP), lambda i: (0, 0))],
        out_specs=pl.BlockSpec((n2, tk1, GROUP_WIDTH), lambda i: (0, i, 0)),
        out_shape=jax.ShapeDtypeStruct((n2, n1, GROUP_WIDTH), BF16),
        compiler_params=_cparams("arbitrary"),
        name="fourier_stage2",
    )(z, twr, twi, f2, cs)
    return out.reshape(n, GROUP_WIDTH)


def _layernorm(z, g, b):
    mu = jnp.mean(z, axis=-1, keepdims=True)
    zc = z - mu
    var = jnp.mean(zc * zc, axis=-1, keepdims=True)
    return zc * lax.rsqrt(var + LN_EPS) * g + b


def _outproj_kernel(da_ref, lru_ref, mla_ref, fft_ref, x_ref, gate_ref, w_ref, g_ref, b_ref, o_ref, *, alpha):
    y = jnp.dot(da_ref[...], w_ref[0:GROUP_WIDTH], preferred_element_type=F32)
    y += jnp.dot(lru_ref[...], w_ref[GROUP_WIDTH:2 * GROUP_WIDTH], preferred_element_type=F32)
    y += jnp.dot(mla_ref[...], w_ref[2 * GROUP_WIDTH:3 * GROUP_WIDTH], preferred_element_type=F32)
    y += jnp.dot(fft_ref[...], w_ref[3 * GROUP_WIDTH:4 * GROUP_WIDTH], preferred_element_type=F32)
    z = alpha * x_ref[...] + gate_ref[...] * y
    o_ref[...] = _layernorm(z, g_ref[...], b_ref[...])


def _outproj_ln(parts, x, gate, w_out, layer, ln_g, ln_b, alpha, tm):
    n, d = x.shape
    tm = _row_tile(n, tm)
    row = lambda i: (i, 0)
    fixed = lambda i: (0, 0)
    return pl.pallas_call(
        functools.partial(_outproj_kernel, alpha=alpha),
        grid=(n // tm,),
        in_specs=[pl.BlockSpec((tm, GROUP_WIDTH), row)] * 4 + [
            pl.BlockSpec((tm, d), row),
            pl.BlockSpec((1, d), fixed),
            pl.BlockSpec((None,) + w_out.shape[1:], lambda i: (layer, 0, 0)),
            pl.BlockSpec((1, d), fixed),
            pl.BlockSpec((1, d), fixed),
        ],
        out_specs=pl.BlockSpec((tm, d), row),
        out_shape=jax.ShapeDtypeStruct((n, d), F32),
        compiler_params=_cparams("arbitrary"),
        name="out_proj_ln",
    )(*parts, x, gate, w_out, ln_g, ln_b)


def _ffn_kernel(x_ref, sh_ref, sc_ref, gate_ref, wg_ref, wu_ref, wd_ref, g_ref, b_ref, o_ref,
                h_ref, acc_ref, *, alpha):
    f = pl.program_id(1)

    @pl.when(f == 0)
    def _():
        h_ref[...] = (x_ref[...] * (1.0 + sc_ref[...]) + sh_ref[...]).astype(BF16)
        acc_ref[...] = jnp.zeros(acc_ref.shape, F32)

    h = h_ref[...]
    a = jnp.dot(h, wg_ref[...], preferred_element_type=F32)
    u = jnp.dot(h, wu_ref[...], preferred_element_type=F32)
    act = (a * jax.nn.sigmoid(a) * u).astype(BF16)
    acc_ref[...] += jnp.dot(act, wd_ref[...], preferred_element_type=F32)

    @pl.when(f == pl.num_programs(1) - 1)
    def _():
        z = alpha * x_ref[...] + gate_ref[...] * acc_ref[...]
        o_ref[...] = _layernorm(z, g_ref[...], b_ref[...])


def _ffn_ln(x, shift, scale, gate, wg, wu, wd, layer, ln_g, ln_b, alpha, tm, tf):
    n, d = x.shape
    d_ff = wg.shape[2]
    tm = _row_tile(n, tm)
    assert d_ff % tf == 0
    row = lambda i, f: (i, 0)
    fixed = lambda i, f: (0, 0)
    return pl.pallas_call(
        functools.partial(_ffn_kernel, alpha=alpha),
        grid=(n // tm, d_ff // tf),
        in_specs=[
            pl.BlockSpec((tm, d), row),
            pl.BlockSpec((1, d), fixed),
            pl.BlockSpec((1, d), fixed),
            pl.BlockSpec((1, d), fixed),
            pl.BlockSpec((None, d, tf), lambda i, f: (layer, 0, f)),
            pl.BlockSpec((None, d, tf), lambda i, f: (layer, 0, f)),
            pl.BlockSpec((None, tf, d), lambda i, f: (layer, f, 0)),
            pl.BlockSpec((1, d), fixed),
            pl.BlockSpec((1, d), fixed),
        ],
        out_specs=pl.BlockSpec((tm, d), row),
        out_shape=jax.ShapeDtypeStruct((n, d), F32),
        scratch_shapes=[pltpu.VMEM((tm, d), BF16), pltpu.VMEM((tm, d), F32)],
        compiler_params=_cparams("arbitrary", "arbitrary"),
        name="ffn_ln",
    )(x, shift, scale, gate, wg, wu, wd, ln_g, ln_b)


def _pad_in_weight(w_in):
    src_fft = 3 * GROUP_WIDTH + 2 * GROUP_WIDTH + MLA_Q_RANK + MLA_KV_RANK + MLA_ROPE
    pad = jnp.zeros(w_in.shape[:2] + (SEG_FFT - (SEG_MLA + MLA_Q_RANK + MLA_KV_RANK + MLA_ROPE),), BF16)
    w = w_in.astype(BF16)
    return jnp.concatenate([w[..., :src_fft], pad, w[..., src_fft:]], axis=-1)


def _pad_mla_weights(w_uq, w_ukv):
    qr = w_uq.shape[0]
    wq = w_uq.reshape(qr, HEADS, MLA_NOPE + MLA_ROPE)
    wq = jnp.concatenate([wq, jnp.zeros((qr, HEADS, MLA_HEAD_PAD - MLA_NOPE - MLA_ROPE), w_uq.dtype)], axis=2)
    wkv = w_ukv.reshape(w_ukv.shape[0], HEADS, MLA_NOPE + MLA_V)
    wk = wkv[:, :, :MLA_NOPE].reshape(w_ukv.shape[0], HEADS * MLA_NOPE)
    wv = wkv[:, :, MLA_NOPE:].reshape(w_ukv.shape[0], HEADS * MLA_V)
    return (wq.reshape(qr, HEADS * MLA_HEAD_PAD).astype(BF16), wk.astype(BF16), wv.astype(BF16))


def _rope_tables(n):
    rows = n // GRID_W
    row = jnp.repeat(jnp.arange(rows, dtype=F32), GRID_W)[:, None]
    col = jnp.tile(jnp.arange(GRID_W, dtype=F32), rows)[:, None]
    axis_dim = ROT_DIM // 2
    inv = ROPE_BASE ** (-jnp.arange(0, axis_dim, 2, dtype=F32) / axis_dim)
    zero = jnp.zeros_like(inv)
    reps = 2 * (LANES // ROT_DIM)
    inv_row = jnp.concatenate([inv, zero] * reps)[None, :]
    inv_col = jnp.concatenate([zero, inv] * reps)[None, :]
    sign = jnp.concatenate([-jnp.ones((ROT_HALF,), F32), jnp.ones((ROT_HALF,), F32)] * (LANES // ROT_DIM))[None, :]
    ang = row * inv_row + col * inv_col
    return jnp.cos(ang), jnp.sin(ang) * sign


def kernel(x, c, ctx, c_ctx, w_ada, b_ada, w_in, w_out, ln1_g, ln1_b, ln2_g, ln2_b,
           da_lq1, da_lk1, da_lq2, da_lk2, da_subln_g,
           lru_conv_w, lru_conv_b, lru_wr, lru_br, lru_wi, lru_bi, lru_lam,
           mla_qn_g, mla_wuq, mla_kvn_g, mla_wukv,
           ffn_wg, ffn_wu, ffn_wd):
    assert x.shape[0] == 1 and c.shape[0] == 1 and ctx.shape[0] == 1
    depth = w_ada.shape[0]
    n, d = x.shape[1], x.shape[2]
    nc = ctx.shape[1]
    alpha = (2 * depth) ** 0.25
    x_lat, x_ctx = x[0], ctx[0]

    cos_l, sin_l = _rope_tables(n)
    cos_c, sin_c = jnp.ones((nc, LANES), F32), jnp.zeros((nc, LANES), F32)

    c8 = jnp.concatenate([c, c_ctx[None], jnp.zeros((SUBLANES - 2, d), F32)], axis=0)
    mods = _ada_mods(c8, w_ada, b_ada)

    w_in_p = _pad_in_weight(w_in)
    w_out_b = w_out.astype(BF16)
    wg_b, wu_b, wd_b = ffn_wg.astype(BF16), ffn_wu.astype(BF16), ffn_wd.astype(BF16)

    row2 = lambda v: v.reshape(1, -1)
    for l in range(depth):
        need_ctx = l < depth - 1
        lambda_init = 0.8 - 0.6 * math.exp(-0.3 * l)
        m_lat = [mods[l, 0:1, i * d:(i + 1) * d] for i in range(6)]
        m_ctx = [mods[l, 1:2, i * d:(i + 1) * d] for i in range(6)]

        wq_p, wk_p, wv_p = _pad_mla_weights(mla_wuq[l], mla_wukv[l])

        ql, kl, vl, lru_l, mla_l, fft_l = _inproj(x_lat, m_lat[0], m_lat[1], cos_l, sin_l, w_in_p, l, 512)
        qc, kc, vc, lru_c, mla_c, fft_c = _inproj(x_ctx, m_ctx[0], m_ctx[1], cos_c, sin_c, w_in_p, l, 256)

        lams = [row2(da_lq1[l]), row2(da_lk1[l]), row2(da_lq2[l]), row2(da_lk2[l])]
        g_da = row2(da_subln_g[l])
        da_l = _diff_attention(ql, kl, vl, (kc, vc), lams, g_da, lambda_init, ATTN_TQ, ATTN_TK, ATTN_CHUNK)

        lru_args = lambda dr: (lru_conv_w[l], row2(lru_conv_b[l]), lru_wr[l, dr].astype(BF16), row2(lru_br[l, dr]),
                               lru_wi[l, dr].astype(BF16), row2(lru_bi[l, dr]), row2(lru_lam[l, dr]))
        h0 = jnp.zeros((1, GROUP_WIDTH), F32)
        hc_f, s_f = _lru_pass(lru_c, *lru_args(0), h0, None, False, 256)
        lru_yc, s_b = _lru_pass(lru_c, *lru_args(1), h0, hc_f, True, 256)
        hl_f, _ = _lru_pass(lru_l, *lru_args(0), s_f, None, False, 256)
        lru_yl, _ = _lru_pass(lru_l, *lru_args(1), s_b, hl_f, True, 256)

        mq_l, mk_l, mv_l = _mla_prep(mla_l, cos_l, sin_l, row2(mla_qn_g[l]), row2(mla_kvn_g[l]), wq_p, wk_p, wv_p, 512)
        mq_c, mk_c, mv_c = _mla_prep(mla_c, cos_c, sin_c, row2(mla_qn_g[l]), row2(mla_kvn_g[l]), wq_p, wk_p, wv_p, 256)
        mla_yl = _mla_attention(mq_l, mk_l, mv_l, (mk_c, mv_c), 2 * ATTN_TQ, ATTN_TK, ATTN_CHUNK)

        fft_yl = _fourier_long(fft_l)

        x_lat = _outproj_ln([da_l, lru_yl, mla_yl, fft_yl], x_lat, m_lat[2], w_out_b, l,
                            row2(ln1_g[l]), row2(ln1_b[l]), alpha, 512)
        x_lat = _ffn_ln(x_lat, m_lat[3], m_lat[4], m_lat[5], wg_b, wu_b, wd_b, l,
                        row2(ln2_g[l]), row2(ln2_b[l]), alpha, 512, 512)

        if need_ctx:
            da_c = _diff_attention(qc, kc, vc, None, lams, g_da, lambda_init, 256, 256, 256)
            mla_yc = _mla_attention(mq_c, mk_c, mv_c, None, 256, 256, 256)
            fft_yc = _fourier_small(fft_c)
            x_ctx = _outproj_ln([da_c, lru_yc, mla_yc, fft_yc], x_ctx, m_ctx[2], w_out_b, l,
                                row2(ln1_g[l]), row2(ln1_b[l]), alpha, 256)
            x_ctx = _ffn_ln(x_ctx, m_ctx[3], m_ctx[4], m_ctx[5], wg_b, wu_b, wd_b, l,
                            row2(ln2_g[l]), row2(ln2_b[l]), alpha, 256, 512)
    return x_lat[None]
```

```python
import functools
import math

import numpy as np
import jax
import jax.numpy as jnp
from jax import lax
from jax.experimental import pallas as pl
from jax.experimental.pallas import tpu as pltpu

F32 = jnp.float32
BF16 = jnp.bfloat16

D_MODEL = 2048
GRID_W = 64
ROPE_BASE = 10000.0
LN_EPS = 1e-5
RMS_EPS = 1e-6

HEADS = 4
DA_HEAD_DIM = 64
GROUP_WIDTH = 512
LRU_BLOCKS = 4
LRU_BLOCK = GROUP_WIDTH // LRU_BLOCKS
CONV_W = 4
CONV_LEFT = 2
LRU_C = 8.0
MLA_Q_RANK = 384
MLA_KV_RANK = 256
MLA_NOPE = 128
MLA_ROPE = 64
MLA_V = 128
FFT_GROUPS = 4
FFT_GROUP = 128
ROT_DIM = 64
ROT_HALF = ROT_DIM // 2

LANES = 128
SUBLANES = 8
SEG_Q = 0
SEG_K = 512
SEG_V = 1024
SEG_LRU = 1536
SEG_MLA = 2560
SEG_FFT = 3328
IN_PAD = 3840
MLA_SEG = 768
MLA_HEAD_PAD = 256

FFT_N1 = 128

VMEM_LIMIT = 56 * 1024 * 1024
LOG2E = math.log2(math.e)

ATTN_TQ = 1024
ATTN_TK = 8192
ATTN_CHUNK = 512
ONES_ROWS = 16
FFN_SPLIT = 2
MAX_EXCESS = 32.0


def _cparams(*sem):
    return pltpu.CompilerParams(dimension_semantics=sem, vmem_limit_bytes=VMEM_LIMIT)


def _row_tile(n, want):
    t = min(n, want)
    assert n % t == 0, (n, t)
    return t


def _ada_kernel(c_ref, w_ref, b_ref, o_ref):
    s = c_ref[...]
    s = s * jax.nn.sigmoid(s)
    o_ref[...] = jnp.dot(s.astype(BF16), w_ref[...].astype(BF16), preferred_element_type=F32) + b_ref[...]


def _ada_mods(c8, w_ada, b_ada):
    depth, d, n6 = w_ada.shape
    tn = 1536
    return pl.pallas_call(
        _ada_kernel,
        grid=(depth, n6 // tn),
        in_specs=[
            pl.BlockSpec((SUBLANES, d), lambda l, j: (0, 0)),
            pl.BlockSpec((None, d, tn), lambda l, j: (l, 0, j)),
            pl.BlockSpec((None, 1, tn), lambda l, j: (l, 0, j)),
        ],
        out_specs=pl.BlockSpec((None, SUBLANES, tn), lambda l, j: (l, 0, j)),
        out_shape=jax.ShapeDtypeStruct((depth, SUBLANES, n6), F32),
        compiler_params=_cparams("arbitrary", "arbitrary"),
        name="ada_mods",
    )(c8, w_ada, b_ada.reshape(depth, 1, n6))


def _rope_slab(x, cos, sin_signed):
    lane = lax.broadcasted_iota(jnp.int32, x.shape, 1)
    partner = jnp.where((lane & ROT_HALF) == 0,
                        pltpu.roll(x, LANES - ROT_HALF, 1),
                        pltpu.roll(x, ROT_HALF, 1))
    return x * cos + partner * sin_signed


def _inproj_kernel(x_ref, sh_ref, sc_ref, cos_ref, sin_ref, w_ref,
                   q_ref, k_ref, v_ref, lru_ref, mla_ref, fft_ref, *, q_scale):
    xm = (x_ref[...] * (1.0 + sc_ref[...]) + sh_ref[...]).astype(BF16)
    cos = cos_ref[...]
    sin = sin_ref[...]

    def seg(a, width):
        return jnp.dot(xm, w_ref[:, a:a + width], preferred_element_type=F32)

    q_all = seg(SEG_Q, GROUP_WIDTH)
    k_all = seg(SEG_K, GROUP_WIDTH)
    for h in range(HEADS):
        o = h * LANES
        q_ref[:, o:o + LANES] = (_rope_slab(q_all[:, o:o + LANES], cos, sin) * q_scale).astype(BF16)
        k_ref[:, o:o + LANES] = _rope_slab(k_all[:, o:o + LANES], cos, sin).astype(BF16)
    v_ref[...] = seg(SEG_V, GROUP_WIDTH).T.astype(BF16)
    lru_ref[...] = seg(SEG_LRU, 2 * GROUP_WIDTH)
    mla_ref[...] = seg(SEG_MLA, MLA_SEG)
    fft_ref[...] = seg(SEG_FFT, GROUP_WIDTH).astype(BF16)


def _inproj(x, shift, scale, cos, sin, w_in_p, layer, tm):
    n, d = x.shape
    tm = _row_tile(n, tm)
    row = lambda i: (i, 0)
    col = lambda i: (0, i)
    fixed = lambda i: (0, 0)
    outs = [
        jax.ShapeDtypeStruct((n, GROUP_WIDTH), BF16),
        jax.ShapeDtypeStruct((n, GROUP_WIDTH), BF16),
        jax.ShapeDtypeStruct((GROUP_WIDTH, n), BF16),
        jax.ShapeDtypeStruct((n, 2 * GROUP_WIDTH), F32),
        jax.ShapeDtypeStruct((n, MLA_SEG), F32),
        jax.ShapeDtypeStruct((n, GROUP_WIDTH), BF16),
    ]
    out_specs = [pl.BlockSpec((tm, o.shape[1]), row) for o in outs]
    out_specs[2] = pl.BlockSpec((GROUP_WIDTH, tm), col)
    return pl.pallas_call(
        functools.partial(_inproj_kernel, q_scale=DA_HEAD_DIM ** -0.5 * LOG2E),
        grid=(n // tm,),
        in_specs=[
            pl.BlockSpec((tm, d), row),
            pl.BlockSpec((1, d), fixed),
            pl.BlockSpec((1, d), fixed),
            pl.BlockSpec((tm, LANES), row),
            pl.BlockSpec((tm, LANES), row),
            pl.BlockSpec((None, d, IN_PAD), lambda i: (layer, 0, 0)),
        ],
        out_specs=out_specs,
        out_shape=outs,
        compiler_params=_cparams("arbitrary"),
        name="in_proj",
    )(x, shift, scale, cos, sin, w_in_p)


def _scores_t(q_t, k):
    return jnp.dot(k, q_t, preferred_element_type=F32)


def _probs_t(s_t, ref_max):
    return jnp.exp2(s_t - ref_max).astype(BF16)


def _weighted_values_t(v_t, p_t):
    v_ext = jnp.concatenate([v_t, jnp.ones((ONES_ROWS, v_t.shape[1]), BF16)], axis=0)
    return jnp.dot(v_ext, p_t, preferred_element_type=F32)


def _softmax_update(s_t, v_t, m_ref, acc_ref, c):
    m_prev = m_ref[c]
    m_new = jnp.maximum(m_prev, jnp.max(s_t, axis=0, keepdims=True))
    alpha = jnp.exp2(m_prev - m_new)
    acc_ref[c] = alpha * acc_ref[c] + _weighted_values_t(v_t, _probs_t(s_t, m_new))
    m_ref[c] = m_new


def _flash_kernel(*refs, n_comp, has_prefix, chunk, epilogue):
    refs = list(refs)
    q_ref = refs.pop(0)
    if has_prefix:
        kc_ref, vtc_ref = refs.pop(0), refs.pop(0)
    k_ref, vt_ref = refs.pop(0), refs.pop(0)
    qt_ref, m_ref, acc_ref = refs[-3:]
    o_ref = refs[-4]
    extra = refs[:-4]
    j = pl.program_id(2)

    @pl.when(j == 0)
    def _():
        m_ref[...] = jnp.full(m_ref.shape, -jnp.inf, F32)
        acc_ref[...] = jnp.zeros(acc_ref.shape, F32)
        q_t = q_ref[...].astype(F32).T
        if n_comp == 2:
            row = lax.broadcasted_iota(jnp.int32, q_t.shape, 0)
            qt_ref[0] = jnp.where(row < DA_HEAD_DIM, q_t, 0.0).astype(BF16)
            qt_ref[1] = jnp.where(row >= DA_HEAD_DIM, q_t, 0.0).astype(BF16)
        else:
            qt_ref[0] = q_t.astype(BF16)
        if has_prefix:
            for c in range(n_comp):
                _softmax_update(_scores_t(qt_ref[c], kc_ref[...]), vtc_ref[...], m_ref, acc_ref, c)

    n_chunks = k_ref.shape[0] // chunk

    def exact_step():
        def body(i, carry):
            off = pl.multiple_of(i * chunk, chunk)
            for c in range(n_comp):
                _softmax_update(_scores_t(qt_ref[c], k_ref[pl.ds(off, chunk), :]), vt_ref[:, pl.ds(off, chunk)],
                                m_ref, acc_ref, c)
            return carry
        lax.fori_loop(0, n_chunks, body, 0)

    if not has_prefix:
        exact_step()
    else:
        ref_max = [m_ref[c] for c in range(n_comp)]
        step_acc = [None] * n_comp
        step_max = [None] * n_comp
        for i in range(n_chunks):
            off = i * chunk
            for c in range(n_comp):
                s_t = _scores_t(qt_ref[c], k_ref[off:off + chunk, :])
                unit_max = jnp.max(s_t, axis=0, keepdims=True)
                pv = _weighted_values_t(vt_ref[:, off:off + chunk], _probs_t(s_t, ref_max[c]))
                step_acc[c] = pv if step_acc[c] is None else step_acc[c] + pv
                step_max[c] = unit_max if step_max[c] is None else jnp.maximum(step_max[c], unit_max)
        excess = jnp.max(step_max[0] - ref_max[0])
        for c in range(1, n_comp):
            excess = jnp.maximum(excess, jnp.max(step_max[c] - ref_max[c]))
        in_range = excess <= MAX_EXCESS

        @pl.when(in_range)
        def _():
            for c in range(n_comp):
                m_new = jnp.maximum(ref_max[c], step_max[c])
                acc_ref[c] = (acc_ref[c] + step_acc[c]) * jnp.exp2(ref_max[c] - m_new)
                m_ref[c] = m_new

        @pl.when(jnp.logical_not(in_range))
        def _():
            exact_step()

    @pl.when(j == pl.num_programs(2) - 1)
    def _():
        epilogue(extra, o_ref, acc_ref)


def _normalised_t(acc_ref, c):
    dv = acc_ref.shape[1] - ONES_ROWS
    return acc_ref[c, 0:dv, :] / acc_ref[c, dv:dv + 1, :]


def _da_epilogue(extra, o_ref, acc_ref, *, lambda_init):
    lq1_ref, lk1_ref, lq2_ref, lk2_ref, g_ref = extra
    lam = (jnp.exp(jnp.sum(lq1_ref[...] * lk1_ref[...], keepdims=True))
           - jnp.exp(jnp.sum(lq2_ref[...] * lk2_ref[...], keepdims=True)) + lambda_init)
    o = (_normalised_t(acc_ref, 0) - lam * _normalised_t(acc_ref, 1)).T
    inv = lax.rsqrt(jnp.mean(o * o, axis=-1, keepdims=True) + RMS_EPS)
    o_ref[...] = ((o * inv * g_ref[...]) * (1.0 - lambda_init)).astype(o_ref.dtype)


def _mla_epilogue(extra, o_ref, acc_ref):
    o_ref[...] = _normalised_t(acc_ref, 0).T.astype(o_ref.dtype)


def _flash_attention(q, k, v_t, prefix, extra, extra_specs, *, dk, n_comp, epilogue, tq, tk, chunk, name):
    nq, nk = q.shape[0], k.shape[0]
    dv = v_t.shape[0] // HEADS
    tq, tk = _row_tile(nq, tq), _row_tile(nk, tk)
    chunk = _row_tile(tk, chunk)
    has_prefix = prefix is not None
    qspec = pl.BlockSpec((tq, dk), lambda h, i, j: (i, h))
    in_specs, args = [qspec], [q]
    if has_prefix:
        nc = prefix[0].shape[0]
        in_specs += [pl.BlockSpec((nc, dk), lambda h, i, j: (0, h)),
                     pl.BlockSpec((dv, nc), lambda h, i, j: (h, 0))]
        args += list(prefix)
    in_specs += [pl.BlockSpec((tk, dk), lambda h, i, j: (j, h)),
                 pl.BlockSpec((dv, tk), lambda h, i, j: (h, j))] + list(extra_specs)
    args += [k, v_t] + list(extra)
    return pl.pallas_call(
        functools.partial(_flash_kernel, n_comp=n_comp, has_prefix=has_prefix, chunk=chunk, epilogue=epilogue),
        grid=(HEADS, nq // tq, nk // tk),
        in_specs=in_specs,
        out_specs=pl.BlockSpec((tq, dv), lambda h, i, j: (i, h)),
        out_shape=jax.ShapeDtypeStruct((nq, HEADS * dv), BF16),
        scratch_shapes=[
            pltpu.VMEM((n_comp, dk, tq), BF16),
            pltpu.VMEM((n_comp, 1, tq), F32),
            pltpu.VMEM((n_comp, dv + ONES_ROWS, tq), F32),
        ],
        compiler_params=_cparams("arbitrary", "arbitrary", "arbitrary"),
        name=name,
    )(*args)


def _diff_attention(q, k, v_t, prefix, lams, g, lambda_init, tq, tk, chunk):
    small = lambda w: pl.BlockSpec((1, w), lambda h, i, j: (0, 0))
    return _flash_attention(
        q, k, v_t, prefix, list(lams) + [g], [small(DA_HEAD_DIM)] * 4 + [small(LANES)],
        dk=2 * DA_HEAD_DIM, n_comp=2, epilogue=functools.partial(_da_epilogue, lambda_init=lambda_init),
        tq=tq, tk=tk, chunk=chunk, name="diff_attention")


def _mla_attention(q, k, v_t, prefix, tq, tk, chunk):
    return _flash_attention(q, k, v_t, prefix, [], [], dk=MLA_HEAD_PAD, n_comp=1, epilogue=_mla_epilogue,
                            tq=tq, tk=tk, chunk=chunk, name="mla_attention")


def _rms(x, g):
    inv = lax.rsqrt(jnp.mean(x * x, axis=-1, keepdims=True) + RMS_EPS)
    return x * inv * g


def _mla_prep_kernel(u_ref, cos_ref, sin_ref, qg_ref, kvg_ref, wq_ref, wk_ref, wv_ref,
                     q_ref, k_ref, v_ref, *, scale):
    cos = cos_ref[...]
    sin = sin_ref[...]
    cq = _rms(u_ref[:, 0:MLA_Q_RANK], qg_ref[...]).astype(BF16)
    ckv = _rms(u_ref[:, MLA_Q_RANK:MLA_Q_RANK + MLA_KV_RANK], kvg_ref[...]).astype(BF16)
    k_rope = _rope_slab(u_ref[:, MLA_Q_RANK + MLA_KV_RANK:MLA_SEG], cos, sin).astype(BF16)
    k_nope = jnp.dot(ckv, wk_ref[...], preferred_element_type=F32)
    for h in range(HEADS):
        o = h * MLA_HEAD_PAD
        q_h = jnp.dot(cq, wq_ref[:, o:o + MLA_HEAD_PAD], preferred_element_type=F32)
        q_ref[:, o:o + LANES] = (q_h[:, :LANES] * scale).astype(BF16)
        q_ref[:, o + LANES:o + 2 * LANES] = (_rope_slab(q_h[:, LANES:], cos, sin) * scale).astype(BF16)
        k_ref[:, o:o + LANES] = k_nope[:, h * LANES:(h + 1) * LANES].astype(BF16)
        k_ref[:, o + LANES:o + 2 * LANES] = k_rope
    v_ref[...] = jnp.dot(ckv, wv_ref[...], preferred_element_type=F32).T.astype(BF16)


def _mla_prep(u_mla, cos, sin, qn_g, kvn_g, wq_p, wk_p, wv_p, tm):
    n = u_mla.shape[0]
    tm = _row_tile(n, tm)
    row = lambda i: (i, 0)
    fixed = lambda i: (0, 0)
    outs = [
        jax.ShapeDtypeStruct((n, HEADS * MLA_HEAD_PAD), BF16),
        jax.ShapeDtypeStruct((n, HEADS * MLA_HEAD_PAD), BF16),
        jax.ShapeDtypeStruct((GROUP_WIDTH, n), BF16),
    ]
    out_specs = [pl.BlockSpec((tm, HEADS * MLA_HEAD_PAD), row)] * 2 + [pl.BlockSpec((GROUP_WIDTH, tm), lambda i: (0, i))]
    return pl.pallas_call(
        functools.partial(_mla_prep_kernel, scale=(MLA_NOPE + MLA_ROPE) ** -0.5 * LOG2E),
        grid=(n // tm,),
        in_specs=[
            pl.BlockSpec((tm, MLA_SEG), row),
            pl.BlockSpec((tm, LANES), row),
            pl.BlockSpec((tm, LANES), row),
            pl.BlockSpec((1, MLA_Q_RANK), fixed),
            pl.BlockSpec((1, MLA_KV_RANK), fixed),
            pl.BlockSpec(wq_p.shape, fixed),
            pl.BlockSpec(wk_p.shape, fixed),
            pl.BlockSpec(wv_p.shape, fixed),
        ],
        out_specs=out_specs,
        out_shape=outs,
        compiler_params=_cparams("arbitrary"),
        name="mla_prep",
    )(u_mla, cos, sin, qn_g, kvn_g, wq_p, wk_p, wv_p)


def _shift_rows(x, d, halo):
    tt = x.shape[0]
    row8 = lax.broadcasted_iota(jnp.int32, (SUBLANES, x.shape[1]), 0)
    if d > 0:
        r = pltpu.roll(x, d, 0)
        f = pltpu.roll(halo, d, 0)
        first = jnp.where(row8 < d, f, r[:SUBLANES])
        return jnp.concatenate([first, r[SUBLANES:]], axis=0) if tt > SUBLANES else first
    r = pltpu.roll(x, tt + d, 0)
    f = pltpu.roll(halo, SUBLANES + d, 0)
    last = jnp.where(row8 >= SUBLANES + d, f, r[tt - SUBLANES:])
    return jnp.concatenate([r[:tt - SUBLANES], last], axis=0) if tt > SUBLANES else last


def _lru_kernel(*refs, reverse, final):
    if final:
        (x_ref, prev_ref, next_ref, cw_ref, cb_ref, wr_ref, br_ref, wi_ref, bi_ref, lam_ref, h0_ref,
         gate_ref, hf_ref, y_ref, hlast_ref, carry_ref) = refs
    else:
        (x_ref, prev_ref, next_ref, cw_ref, cb_ref, wr_ref, br_ref, wi_ref, bi_ref, lam_ref, h0_ref,
         y_ref, hlast_ref, carry_ref) = refs
    i = pl.program_id(0)
    nt = pl.num_programs(0)
    t = (nt - 1 - i) if reverse else i
    x = x_ref[...]
    tt = x.shape[0]

    @pl.when(i == 0)
    def _():
        carry_ref[...] = h0_ref[...]

    prev = prev_ref[...] * (t > 0).astype(F32)
    nxt = next_ref[...] * (t < nt - 1).astype(F32)
    cw = cw_ref[...]
    xc = (cw[0:1] * _shift_rows(x, 2, prev) + cw[1:2] * _shift_rows(x, 1, prev)
          + cw[2:3] * x + cw[3:4] * _shift_rows(x, -1, nxt)) + cb_ref[...]

    xb = xc.astype(BF16)

    def gate(w_ref, b_ref):
        z = jnp.concatenate(
            [jnp.dot(xb[:, b * LRU_BLOCK:(b + 1) * LRU_BLOCK], w_ref[b], preferred_element_type=F32)
             for b in range(LRU_BLOCKS)], axis=1)
        return jax.nn.sigmoid(z + b_ref[...])

    r = gate(wr_ref, br_ref)
    ig = gate(wi_ref, bi_ref)
    log_a = -LRU_C * r * jax.nn.softplus(-lam_ref[...])
    a = jnp.exp(log_a)
    u = jnp.sqrt(1.0 - jnp.exp(2.0 * log_a)) * (ig * xc)

    row = lax.broadcasted_iota(jnp.int32, a.shape, 0)
    big_a, big_b = a, u
    d = 1
    while d < tt:
        if reverse:
            valid = row < tt - d
            a_s = pltpu.roll(big_a, tt - d, 0)
            b_s = pltpu.roll(big_b, tt - d, 0)
        else:
            valid = row >= d
            a_s = pltpu.roll(big_a, d, 0)
            b_s = pltpu.roll(big_b, d, 0)
        big_b = jnp.where(valid, big_a * b_s + big_b, big_b)
        big_a = jnp.where(valid, big_a * a_s, big_a)
        d *= 2
    h = big_a * carry_ref[...] + big_b
    edge = h[0:1] if reverse else h[tt - 1:tt]
    carry_ref[...] = edge
    hlast_ref[...] = edge
    if final:
        y_ref[...] = ((hf_ref[...] + h) * jax.nn.gelu(gate_ref[...])).astype(y_ref.dtype)
    else:
        y_ref[...] = h


def _lru_pass(u_lru, conv_w, conv_b, wr, br, wi, bi, lam, h0, h_fwd, reverse, tt):
    n = u_lru.shape[0]
    tt = _row_tile(n, tt)
    nt = n // tt
    per8 = tt // SUBLANES
    final = h_fwd is not None
    pos = (lambda i: nt - 1 - i) if reverse else (lambda i: i)
    tile = lambda i: (pos(i), 0)
    fixed = lambda i: (0, 0)
    fixed3 = lambda i: (0, 0, 0)
    in_specs = [
        pl.BlockSpec((tt, GROUP_WIDTH), tile),
        pl.BlockSpec((SUBLANES, GROUP_WIDTH), lambda i: (jnp.maximum(pos(i) * per8 - 1, 0), 0)),
        pl.BlockSpec((SUBLANES, GROUP_WIDTH), lambda i: (jnp.minimum((pos(i) + 1) * per8, n // SUBLANES - 1), 0)),
        pl.BlockSpec((CONV_W, GROUP_WIDTH), fixed),
        pl.BlockSpec((1, GROUP_WIDTH), fixed),
        pl.BlockSpec((LRU_BLOCKS, LRU_BLOCK, LRU_BLOCK), fixed3),
        pl.BlockSpec((1, GROUP_WIDTH), fixed),
        pl.BlockSpec((LRU_BLOCKS, LRU_BLOCK, LRU_BLOCK), fixed3),
        pl.BlockSpec((1, GROUP_WIDTH), fixed),
        pl.BlockSpec((1, GROUP_WIDTH), fixed),
        pl.BlockSpec((1, GROUP_WIDTH), fixed),
    ]
    args = [u_lru, u_lru, u_lru, conv_w, conv_b, wr, br, wi, bi, lam, h0]
    if final:
        in_specs += [pl.BlockSpec((tt, GROUP_WIDTH), lambda i: (pos(i), 1)),
                     pl.BlockSpec((tt, GROUP_WIDTH), tile)]
        args += [u_lru, h_fwd]
    return pl.pallas_call(
        functools.partial(_lru_kernel, reverse=reverse, final=final),
        grid=(nt,),
        in_specs=in_specs,
        out_specs=[pl.BlockSpec((tt, GROUP_WIDTH), tile), pl.BlockSpec((1, GROUP_WIDTH), fixed)],
        out_shape=[jax.ShapeDtypeStruct((n, GROUP_WIDTH), BF16 if final else F32),
                   jax.ShapeDtypeStruct((1, GROUP_WIDTH), F32)],
        scratch_shapes=[pltpu.VMEM((1, GROUP_WIDTH), F32)],
        compiler_params=_cparams("arbitrary"),
        name="rglru_bwd" if reverse else "rglru_fwd",
    )(*args)


def _dft_tables(n):
    k = np.arange(n, dtype=np.int64)
    ang = 2.0 * np.pi * ((k[:, None] * k[None, :]) % n).astype(np.float64) / n
    return np.cos(ang), np.sin(ang)


def _fft_small_kernel(g_ref, cs_ref, cn_ref, sn_ref, o_ref, *, norm):
    g = g_ref[...]
    a_parts, b_parts = [], []
    for grp in range(FFT_GROUPS):
        gg = g[:, grp * FFT_GROUP:(grp + 1) * FFT_GROUP]
        ab = jnp.dot(gg, cs_ref[...], preferred_element_type=F32)
        a_parts.append(ab[:, :FFT_GROUP])
        b_parts.append(ab[:, FFT_GROUP:])
    a = jnp.concatenate(a_parts, axis=1).astype(BF16)
    b = jnp.concatenate(b_parts, axis=1).astype(BF16)
    y = (jnp.dot(cn_ref[...], a, preferred_element_type=F32)
         - jnp.dot(sn_ref[...], b, preferred_element_type=F32))
    o_ref[...] = (y * norm).astype(o_ref.dtype)


def _fourier_small(g):
    n = g.shape[0]
    cc, sc = _dft_tables(FFT_GROUP)
    cn, sn = _dft_tables(n)
    cs = jnp.asarray(np.concatenate([cc, sc], axis=1), BF16)
    return pl.pallas_call(
        functools.partial(_fft_small_kernel, norm=float((n * FFT_GROUP) ** -0.5)),
        out_shape=jax.ShapeDtypeStruct((n, GROUP_WIDTH), BF16),
        compiler_params=pltpu.CompilerParams(vmem_limit_bytes=VMEM_LIMIT),
        name="fourier_ctx",
    )(g, cs, jnp.asarray(cn, BF16), jnp.asarray(sn, BF16))


def _fft_stage1_kernel(f_ref, x_ref, z_ref):
    res = jnp.dot(f_ref[...], x_ref[...], preferred_element_type=F32)
    tj = z_ref.shape[2]
    res = res.reshape(2, FFT_N1, tj * GROUP_WIDTH)
    for jj in range(tj):
        z_ref[:, :, jj, :] = res[:, :, jj * GROUP_WIDTH:(jj + 1) * GROUP_WIDTH]


def _fft_stage2_kernel(z_ref, twr_ref, twi_ref, f2_ref, cs_ref, o_ref, *, norm):
    tk1 = z_ref.shape[1]
    for i in range(tk1):
        zr = z_ref[0, i]
        zi = z_ref[1, i]
        twr = jnp.concatenate([twr_ref[i]] * (GROUP_WIDTH // LANES), axis=1)
        twi = jnp.concatenate([twi_ref[i]] * (GROUP_WIDTH // LANES), axis=1)
        zz = jnp.concatenate([zr * twr - zi * twi, zr * twi + zi * twr], axis=0).astype(BF16)
        p = jnp.dot(f2_ref[...], zz, preferred_element_type=F32)
        n2 = p.shape[0] // 2
        pr = p[:n2].astype(BF16)
        pim = p[n2:].astype(BF16)
        outs = []
        for grp in range(FFT_GROUPS):
            sl = slice(grp * FFT_GROUP, (grp + 1) * FFT_GROUP)
            lhs = jnp.concatenate([pr[:, sl], pim[:, sl]], axis=1)
            outs.append(jnp.dot(lhs, cs_ref[...], preferred_element_type=F32))
        o_ref[:, i, :] = (jnp.concatenate(outs, axis=1) * norm).astype(o_ref.dtype)


def _fourier_long(g):
    n = g.shape[0]
    n1 = FFT_N1
    assert n % (n1 * SUBLANES) == 0, n
    n2 = n // n1
    c1, s1 = _dft_tables(n1)
    f1 = jnp.asarray(np.concatenate([c1, -s1], axis=0), BF16)
    tj = min(n2, SUBLANES)
    z = pl.pallas_call(
        _fft_stage1_kernel,
        grid=(n2 // tj,),
        in_specs=[pl.BlockSpec((2 * n1, n1), lambda j: (0, 0)),
                  pl.BlockSpec((n1, tj * GROUP_WIDTH), lambda j: (0, j))],
        out_specs=pl.BlockSpec((2, n1, tj, GROUP_WIDTH), lambda j: (0, 0, j, 0)),
        out_shape=jax.ShapeDtypeStruct((2, n1, n2, GROUP_WIDTH), F32),
        compiler_params=_cparams("arbitrary"),
        name="fourier_stage1",
    )(f1, g.reshape(n1, n2 * GROUP_WIDTH))

    k1 = jnp.arange(n1, dtype=F32)[:, None]
    j2 = jnp.arange(n2, dtype=F32)[None, :]
    ang = (2.0 * np.pi / n) * (k1 * j2)
    twr = jnp.broadcast_to(jnp.cos(ang)[:, :, None], (n1, n2, LANES))
    twi = jnp.broadcast_to(-jnp.sin(ang)[:, :, None], (n1, n2, LANES))
    c2, s2 = _dft_tables(n2)
    f2 = jnp.asarray(np.block([[c2, s2], [-s2, c2]]), BF16)
    cc, sc = _dft_tables(FFT_GROUP)
    cs = jnp.asarray(np.concatenate([cc, sc], axis=0), BF16)
    tk1 = SUBLANES
    out = pl.pallas_call(
        functools.partial(_fft_stage2_kernel, norm=float((n * FFT_GROUP) ** -0.5)),
        grid=(n1 // tk1,),
        in_specs=[pl.BlockSpec((2, tk1, n2, GROUP_WIDTH), lambda i: (0, i, 0, 0)),
                  pl.BlockSpec((tk1, n2, LANES), lambda i: (i, 0, 0)),
                  pl.BlockSpec((tk1, n2, LANES), lambda i: (i, 0, 0)),
                  pl.BlockSpec((2 * n2, 2 * n2), lambda i: (0, 0)),
                  pl.BlockSpec((2 * FFT_GROUP, FFT_GROUP), lambda i: (0, 0))],
        out_specs=pl.BlockSpec((n2, tk1, GROUP_WIDTH), lambda i: (0, i, 0)),
        out_shape=jax.ShapeDtypeStruct((n2, n1, GROUP_WIDTH), BF16),
        compiler_params=_cparams("arbitrary"),
        name="fourier_stage2",
    )(z, twr, twi, f2, cs)
    return out.reshape(n, GROUP_WIDTH)


def _layernorm(z, g, b):
    mu = jnp.mean(z, axis=-1, keepdims=True)
    zc = z - mu
    var = jnp.mean(zc * zc, axis=-1, keepdims=True)
    return zc * lax.rsqrt(var + LN_EPS) * g + b


def _outproj_kernel(da_ref, lru_ref, mla_ref, fft_ref, x_ref, gate_ref, w_ref, g_ref, b_ref, o_ref, *, alpha):
    y = jnp.dot(da_ref[...], w_ref[0:GROUP_WIDTH], preferred_element_type=F32)
    y += jnp.dot(lru_ref[...], w_ref[GROUP_WIDTH:2 * GROUP_WIDTH], preferred_element_type=F32)
    y += jnp.dot(mla_ref[...], w_ref[2 * GROUP_WIDTH:3 * GROUP_WIDTH], preferred_element_type=F32)
    y += jnp.dot(fft_ref[...], w_ref[3 * GROUP_WIDTH:4 * GROUP_WIDTH], preferred_element_type=F32)
    z = alpha * x_ref[...] + gate_ref[...] * y
    o_ref[...] = _layernorm(z, g_ref[...], b_ref[...])


def _outproj_ln(parts, x, gate, w_out, layer, ln_g, ln_b, alpha, tm):
    n, d = x.shape
    tm = _row_tile(n, tm)
    row = lambda i: (i, 0)
    fixed = lambda i: (0, 0)
    return pl.pallas_call(
        functools.partial(_outproj_kernel, alpha=alpha),
        grid=(n // tm,),
        in_specs=[pl.BlockSpec((tm, GROUP_WIDTH), row)] * 4 + [
            pl.BlockSpec((tm, d), row),
            pl.BlockSpec((1, d), fixed),
            pl.BlockSpec((None,) + w_out.shape[1:], lambda i: (layer, 0, 0)),
            pl.BlockSpec((1, d), fixed),
            pl.BlockSpec((1, d), fixed),
        ],
        out_specs=pl.BlockSpec((tm, d), row),
        out_shape=jax.ShapeDtypeStruct((n, d), F32),
        compiler_params=_cparams("arbitrary"),
        name="out_proj_ln",
    )(*parts, x, gate, w_out, ln_g, ln_b)


def _ffn_kernel(x_ref, sh_ref, sc_ref, gate_ref, wg_ref, wu_ref, wd_ref, g_ref, b_ref, o_ref,
                h_ref, acc_ref, *, alpha):
    f = pl.program_id(1)

    @pl.when(f == 0)
    def _():
        h_ref[...] = (x_ref[...] * (1.0 + sc_ref[...]) + sh_ref[...]).astype(BF16)
        acc_ref[...] = jnp.zeros(acc_ref.shape, F32)

    h = h_ref[...]
    sub = wg_ref.shape[1] // FFN_SPLIT
    contrib = None
    for s in range(FFN_SPLIT):
        cols = slice(s * sub, (s + 1) * sub)
        a = jnp.dot(h, wg_ref[:, cols], preferred_element_type=F32)
        u = jnp.dot(h, wu_ref[:, cols], preferred_element_type=F32)
        act = (a * jax.nn.sigmoid(a) * u).astype(BF16)
        part = jnp.dot(act, wd_ref[cols, :], preferred_element_type=F32)
        contrib = part if contrib is None else contrib + part
    acc_ref[...] += contrib

    @pl.when(f == pl.num_programs(1) - 1)
    def _():
        z = alpha * x_ref[...] + gate_ref[...] * acc_ref[...]
        o_ref[...] = _layernorm(z, g_ref[...], b_ref[...])


def _ffn_ln(x, shift, scale, gate, wg, wu, wd, layer, ln_g, ln_b, alpha, tm, tf):
    n, d = x.shape
    d_ff = wg.shape[2]
    tm = _row_tile(n, tm)
    assert d_ff % tf == 0
    row = lambda i, f: (i, 0)
    fixed = lambda i, f: (0, 0)
    return pl.pallas_call(
        functools.partial(_ffn_kernel, alpha=alpha),
        grid=(n // tm, d_ff // tf),
        in_specs=[
            pl.BlockSpec((tm, d), row),
            pl.BlockSpec((1, d), fixed),
            pl.BlockSpec((1, d), fixed),
            pl.BlockSpec((1, d), fixed),
            pl.BlockSpec((None, d, tf), lambda i, f: (layer, 0, f)),
            pl.BlockSpec((None, d, tf), lambda i, f: (layer, 0, f)),
            pl.BlockSpec((None, tf, d), lambda i, f: (layer, f, 0)),
            pl.BlockSpec((1, d), fixed),
            pl.BlockSpec((1, d), fixed),
        ],
        out_specs=pl.BlockSpec((tm, d), row),
        out_shape=jax.ShapeDtypeStruct((n, d), F32),
        scratch_shapes=[pltpu.VMEM((tm, d), BF16), pltpu.VMEM((tm, d), F32)],
        compiler_params=_cparams("arbitrary", "arbitrary"),
        name="ffn_ln",
    )(x, shift, scale, gate, wg, wu, wd, ln_g, ln_b)


def _pad_in_weight(w_in):
    src_fft = 3 * GROUP_WIDTH + 2 * GROUP_WIDTH + MLA_Q_RANK + MLA_KV_RANK + MLA_ROPE
    pad = jnp.zeros(w_in.shape[:2] + (SEG_FFT - (SEG_MLA + MLA_Q_RANK + MLA_KV_RANK + MLA_ROPE),), BF16)
    w = w_in.astype(BF16)
    return jnp.concatenate([w[..., :src_fft], pad, w[..., src_fft:]], axis=-1)


def _pad_mla_weights(w_uq, w_ukv):
    qr = w_uq.shape[0]
    wq = w_uq.reshape(qr, HEADS, MLA_NOPE + MLA_ROPE)
    wq = jnp.concatenate([wq, jnp.zeros((qr, HEADS, MLA_HEAD_PAD - MLA_NOPE - MLA_ROPE), w_uq.dtype)], axis=2)
    wkv = w_ukv.reshape(w_ukv.shape[0], HEADS, MLA_NOPE + MLA_V)
    wk = wkv[:, :, :MLA_NOPE].reshape(w_ukv.shape[0], HEADS * MLA_NOPE)
    wv = wkv[:, :, MLA_NOPE:].reshape(w_ukv.shape[0], HEADS * MLA_V)
    return (wq.reshape(qr, HEADS * MLA_HEAD_PAD).astype(BF16), wk.astype(BF16), wv.astype(BF16))


def _rope_tables(n):
    rows = n // GRID_W
    axis_dim = ROT_DIM // 2
    inv = ROPE_BASE ** (-jnp.arange(0, axis_dim, 2, dtype=F32) / axis_dim)
    ang_row = jnp.arange(rows, dtype=F32)[:, None] * inv
    ang_col = jnp.arange(GRID_W, dtype=F32)[:, None] * inv

    def slab(fn, sign_first):
        by_row = jnp.broadcast_to(fn(ang_row)[:, None, :], (rows, GRID_W, inv.shape[0]))
        by_col = jnp.broadcast_to(fn(ang_col)[None, :, :], (rows, GRID_W, inv.shape[0]))
        half = jnp.concatenate([by_row, by_col], axis=-1).reshape(n, ROT_HALF)
        return jnp.concatenate([sign_first * half, half] * (LANES // ROT_DIM), axis=1)

    return slab(jnp.cos, 1.0), slab(jnp.sin, -1.0)


def kernel(x, c, ctx, c_ctx, w_ada, b_ada, w_in, w_out, ln1_g, ln1_b, ln2_g, ln2_b,
           da_lq1, da_lk1, da_lq2, da_lk2, da_subln_g,
           lru_conv_w, lru_conv_b, lru_wr, lru_br, lru_wi, lru_bi, lru_lam,
           mla_qn_g, mla_wuq, mla_kvn_g, mla_wukv,
           ffn_wg, ffn_wu, ffn_wd):
    assert x.shape[0] == 1 and c.shape[0] == 1 and ctx.shape[0] == 1
    depth = w_ada.shape[0]
    n, d = x.shape[1], x.shape[2]
    nc = ctx.shape[1]
    alpha = (2 * depth) ** 0.25
    x_lat, x_ctx = x[0], ctx[0]

    cos_l, sin_l = _rope_tables(n)
    cos_c, sin_c = jnp.ones((nc, LANES), F32), jnp.zeros((nc, LANES), F32)

    c8 = jnp.concatenate([c, c_ctx[None], jnp.zeros((SUBLANES - 2, d), F32)], axis=0)
    mods = _ada_mods(c8, w_ada, b_ada)

    w_in_p = _pad_in_weight(w_in)
    w_out_b = w_out.astype(BF16)
    wg_b, wu_b, wd_b = ffn_wg.astype(BF16), ffn_wu.astype(BF16), ffn_wd.astype(BF16)

    row2 = lambda v: v.reshape(1, -1)
    for l in range(depth):
        need_ctx = l < depth - 1
        lambda_init = 0.8 - 0.6 * math.exp(-0.3 * l)
        m_lat = [mods[l, 0:1, i * d:(i + 1) * d] for i in range(6)]
        m_ctx = [mods[l, 1:2, i * d:(i + 1) * d] for i in range(6)]

        wq_p, wk_p, wv_p = _pad_mla_weights(mla_wuq[l], mla_wukv[l])

        ql, kl, vl, lru_l, mla_l, fft_l = _inproj(x_lat, m_lat[0], m_lat[1], cos_l, sin_l, w_in_p, l, 512)
        qc, kc, vc, lru_c, mla_c, fft_c = _inproj(x_ctx, m_ctx[0], m_ctx[1], cos_c, sin_c, w_in_p, l, 256)

        lams = [row2(da_lq1[l]), row2(da_lk1[l]), row2(da_lq2[l]), row2(da_lk2[l])]
        g_da = row2(da_subln_g[l])
        da_l = _diff_attention(ql, kl, vl, (kc, vc), lams, g_da, lambda_init, ATTN_TQ, ATTN_TK, ATTN_CHUNK)

        lru_args = lambda dr: (lru_conv_w[l], row2(lru_conv_b[l]), lru_wr[l, dr].astype(BF16), row2(lru_br[l, dr]),
                               lru_wi[l, dr].astype(BF16), row2(lru_bi[l, dr]), row2(lru_lam[l, dr]))
        h0 = jnp.zeros((1, GROUP_WIDTH), F32)
        hc_f, s_f = _lru_pass(lru_c, *lru_args(0), h0, None, False, 256)
        lru_yc, s_b = _lru_pass(lru_c, *lru_args(1), h0, hc_f, True, 256)
        hl_f, _ = _lru_pass(lru_l, *lru_args(0), s_f, None, False, 256)
        lru_yl, _ = _lru_pass(lru_l, *lru_args(1), s_b, hl_f, True, 256)

        mq_l, mk_l, mv_l = _mla_prep(mla_l, cos_l, sin_l, row2(mla_qn_g[l]), row2(mla_kvn_g[l]), wq_p, wk_p, wv_p, 512)
        mq_c, mk_c, mv_c = _mla_prep(mla_c, cos_c, sin_c, row2(mla_qn_g[l]), row2(mla_kvn_g[l]), wq_p, wk_p, wv_p, 256)
        mla_yl = _mla_attention(mq_l, mk_l, mv_l, (mk_c, mv_c), 2 * ATTN_TQ, ATTN_TK, ATTN_CHUNK)

        fft_yl = _fourier_long(fft_l)

        x_lat = _outproj_ln([da_l, lru_yl, mla_yl, fft_yl], x_lat, m_lat[2], w_out_b, l,
                            row2(ln1_g[l]), row2(ln1_b[l]), alpha, 512)
        x_lat = _ffn_ln(x_lat, m_lat[3], m_lat[4], m_lat[5], wg_b, wu_b, wd_b, l,
                        row2(ln2_g[l]), row2(ln2_b[l]), alpha, 512, 512)

        if need_ctx:
            da_c = _diff_attention(qc, kc, vc, None, lams, g_da, lambda_init, 256, 256, 256)
            mla_yc = _mla_attention(mq_c, mk_c, mv_c, None, 256, 256, 256)
            fft_yc = _fourier_small(fft_c)
            x_ctx = _outproj_ln([da_c, lru_yc, mla_yc, fft_yc], x_ctx, m_ctx[2], w_out_b, l,
                                row2(ln1_g[l]), row2(ln1_b[l]), alpha, 256)
            x_ctx = _ffn_ln(x_ctx, m_ctx[3], m_ctx[4], m_ctx[5], wg_b, wu_b, wd_b, l,
                            row2(ln2_g[l]), row2(ln2_b[l]), alpha, 256, 512)
    return x_lat[None]
```

```python
import functools
import math

import numpy as np
import jax
import jax.numpy as jnp
from jax import lax
from jax.experimental import pallas as pl
from jax.experimental.pallas import tpu as pltpu

F32 = jnp.float32
BF16 = jnp.bfloat16

D_MODEL = 2048
GRID_W = 64
ROPE_BASE = 10000.0
LN_EPS = 1e-5
RMS_EPS = 1e-6

HEADS = 4
DA_HEAD_DIM = 64
GROUP_WIDTH = 512
LRU_BLOCKS = 4
LRU_BLOCK = GROUP_WIDTH // LRU_BLOCKS
CONV_W = 4
CONV_LEFT = 2
LRU_C = 8.0
MLA_Q_RANK = 384
MLA_KV_RANK = 256
MLA_NOPE = 128
MLA_ROPE = 64
MLA_V = 128
FFT_GROUPS = 4
FFT_GROUP = 128
ROT_DIM = 64
ROT_HALF = ROT_DIM // 2

LANES = 128
SUBLANES = 8
SEG_Q = 0
SEG_K = 512
SEG_V = 1024
SEG_LRU = 1536
SEG_MLA = 2560
SEG_FFT = 3328
MLA_SEG = 768
MLA_HEAD_PAD = 256

FFT_N1 = 128

VMEM_LIMIT = 56 * 1024 * 1024
LOG2E = math.log2(math.e)

ATTN_TQ = 1024
ATTN_TK = 8192
ATTN_CHUNK = 512
ONES_ROWS = 16
FFN_SUB = 256
MAX_EXCESS = 32.0


def _cparams(*sem):
    return pltpu.CompilerParams(dimension_semantics=sem, vmem_limit_bytes=VMEM_LIMIT)


def _row_tile(n, want):
    t = min(n, want)
    assert n % t == 0, (n, t)
    return t


def _ada_kernel(c_ref, w_ref, b_ref, o_ref):
    s = c_ref[...]
    s = s * jax.nn.sigmoid(s)
    o_ref[...] = jnp.dot(s.astype(BF16), w_ref[...].astype(BF16), preferred_element_type=F32) + b_ref[...]


def _ada_mods(c8, w_ada, b_ada):
    depth, d, n6 = w_ada.shape
    tn = 1536
    return pl.pallas_call(
        _ada_kernel,
        grid=(depth, n6 // tn),
        in_specs=[
            pl.BlockSpec((SUBLANES, d), lambda l, j: (0, 0)),
            pl.BlockSpec((None, d, tn), lambda l, j: (l, 0, j)),
            pl.BlockSpec((None, 1, tn), lambda l, j: (l, 0, j)),
        ],
        out_specs=pl.BlockSpec((None, SUBLANES, tn), lambda l, j: (l, 0, j)),
        out_shape=jax.ShapeDtypeStruct((depth, SUBLANES, n6), F32),
        compiler_params=_cparams("arbitrary", "arbitrary"),
        name="ada_mods",
    )(c8, w_ada, b_ada.reshape(depth, 1, n6))


def _rope_slab(x, cos, sin_signed):
    lane = lax.broadcasted_iota(jnp.int32, x.shape, 1)
    partner = jnp.where((lane & ROT_HALF) == 0,
                        pltpu.roll(x, LANES - ROT_HALF, 1),
                        pltpu.roll(x, ROT_HALF, 1))
    return x * cos + partner * sin_signed


def _inproj_kernel(x_ref, sh_ref, sc_ref, cos_ref, sin_ref, w_ref, wf_ref,
                   q_ref, k_ref, v_ref, lru_ref, mla_ref, fft_ref, *, q_scale):
    xm = (x_ref[...] * (1.0 + sc_ref[...]) + sh_ref[...]).astype(BF16)
    cos = cos_ref[...]
    sin = sin_ref[...]

    def seg(a, width):
        return jnp.dot(xm, w_ref[:, a:a + width], preferred_element_type=F32)

    q_all = seg(SEG_Q, GROUP_WIDTH)
    k_all = seg(SEG_K, GROUP_WIDTH)
    for h in range(HEADS):
        o = h * LANES
        q_ref[:, o:o + LANES] = (_rope_slab(q_all[:, o:o + LANES], cos, sin) * q_scale).astype(BF16)
        k_ref[:, o:o + LANES] = _rope_slab(k_all[:, o:o + LANES], cos, sin).astype(BF16)
    v_ref[...] = seg(SEG_V, GROUP_WIDTH).T.astype(BF16)
    lru_ref[...] = seg(SEG_LRU, 2 * GROUP_WIDTH)
    mla_ref[...] = seg(SEG_MLA, MLA_SEG)
    fft_ref[...] = jnp.dot(xm, wf_ref[...], preferred_element_type=F32).astype(BF16)


def _inproj(x, shift, scale, cos, sin, w_main, w_fft, layer, tm):
    n, d = x.shape
    tm = _row_tile(n, tm)
    row = lambda i: (i, 0)
    col = lambda i: (0, i)
    fixed = lambda i: (0, 0)
    outs = [
        jax.ShapeDtypeStruct((n, GROUP_WIDTH), BF16),
        jax.ShapeDtypeStruct((n, GROUP_WIDTH), BF16),
        jax.ShapeDtypeStruct((GROUP_WIDTH, n), BF16),
        jax.ShapeDtypeStruct((n, 2 * GROUP_WIDTH), F32),
        jax.ShapeDtypeStruct((n, MLA_SEG), F32),
        jax.ShapeDtypeStruct((n, GROUP_WIDTH), BF16),
    ]
    out_specs = [pl.BlockSpec((tm, o.shape[1]), row) for o in outs]
    out_specs[2] = pl.BlockSpec((GROUP_WIDTH, tm), col)
    return pl.pallas_call(
        functools.partial(_inproj_kernel, q_scale=DA_HEAD_DIM ** -0.5 * LOG2E),
        grid=(n // tm,),
        in_specs=[
            pl.BlockSpec((tm, d), row),
            pl.BlockSpec((1, d), fixed),
            pl.BlockSpec((1, d), fixed),
            pl.BlockSpec((tm, LANES), row),
            pl.BlockSpec((tm, LANES), row),
            pl.BlockSpec((None, d, SEG_FFT), lambda i: (layer, 0, 0)),
            pl.BlockSpec((None, d, GROUP_WIDTH), lambda i: (layer, 0, 0)),
        ],
        out_specs=out_specs,
        out_shape=outs,
        compiler_params=_cparams("arbitrary"),
        name="in_proj",
    )(x, shift, scale, cos, sin, w_main, w_fft)


def _scores_t(q_t, k):
    return jnp.dot(k, q_t, preferred_element_type=F32)


def _probs_t(s_t, ref_max):
    return jnp.exp2(s_t - ref_max).astype(BF16)


def _weighted_values_t(v_t, p_t):
    v_ext = jnp.concatenate([v_t, jnp.ones((ONES_ROWS, v_t.shape[1]), BF16)], axis=0)
    return jnp.dot(v_ext, p_t, preferred_element_type=F32)


def _softmax_update(s_t, v_t, m_ref, acc_ref, c):
    m_prev = m_ref[c]
    m_new = jnp.maximum(m_prev, jnp.max(s_t, axis=0, keepdims=True))
    alpha = jnp.exp2(m_prev - m_new)
    acc_ref[c] = alpha * acc_ref[c] + _weighted_values_t(v_t, _probs_t(s_t, m_new))
    m_ref[c] = m_new


def _flash_kernel(*refs, n_comp, has_prefix, chunk, epilogue):
    refs = list(refs)
    q_ref = refs.pop(0)
    if has_prefix:
        kc_ref, vtc_ref = refs.pop(0), refs.pop(0)
    k_ref, vt_ref = refs.pop(0), refs.pop(0)
    qt_ref, m_ref, acc_ref = refs[-3:]
    o_ref = refs[-4]
    extra = refs[:-4]
    j = pl.program_id(2)

    @pl.when(j == 0)
    def _():
        m_ref[...] = jnp.full(m_ref.shape, -jnp.inf, F32)
        acc_ref[...] = jnp.zeros(acc_ref.shape, F32)
        q_t = q_ref[...].astype(F32).T
        if n_comp == 2:
            row = lax.broadcasted_iota(jnp.int32, q_t.shape, 0)
            qt_ref[0] = jnp.where(row < DA_HEAD_DIM, q_t, 0.0).astype(BF16)
            qt_ref[1] = jnp.where(row >= DA_HEAD_DIM, q_t, 0.0).astype(BF16)
        else:
            qt_ref[0] = q_t.astype(BF16)
        if has_prefix:
            for c in range(n_comp):
                _softmax_update(_scores_t(qt_ref[c], kc_ref[...]), vtc_ref[...], m_ref, acc_ref, c)

    n_chunks = k_ref.shape[0] // chunk

    def exact_step():
        def body(i, carry):
            off = pl.multiple_of(i * chunk, chunk)
            for c in range(n_comp):
                _softmax_update(_scores_t(qt_ref[c], k_ref[pl.ds(off, chunk), :]), vt_ref[:, pl.ds(off, chunk)],
                                m_ref, acc_ref, c)
            return carry
        lax.fori_loop(0, n_chunks, body, 0)

    if not has_prefix:
        exact_step()
    else:
        ref_max = [m_ref[c] for c in range(n_comp)]
        step_acc = [None] * n_comp
        step_max = [None] * n_comp
        for i in range(n_chunks):
            off = i * chunk
            for c in range(n_comp):
                s_t = _scores_t(qt_ref[c], k_ref[off:off + chunk, :])
                unit_max = jnp.max(s_t, axis=0, keepdims=True)
                pv = _weighted_values_t(vt_ref[:, off:off + chunk], _probs_t(s_t, ref_max[c]))
                step_acc[c] = pv if step_acc[c] is None else step_acc[c] + pv
                step_max[c] = unit_max if step_max[c] is None else jnp.maximum(step_max[c], unit_max)
        excess = jnp.max(step_max[0] - ref_max[0])
        for c in range(1, n_comp):
            excess = jnp.maximum(excess, jnp.max(step_max[c] - ref_max[c]))
        in_range = excess <= MAX_EXCESS

        @pl.when(in_range)
        def _():
            for c in range(n_comp):
                m_new = jnp.maximum(ref_max[c], step_max[c])
                acc_ref[c] = (acc_ref[c] + step_acc[c]) * jnp.exp2(ref_max[c] - m_new)
                m_ref[c] = m_new

        @pl.when(jnp.logical_not(in_range))
        def _():
            exact_step()

    @pl.when(j == pl.num_programs(2) - 1)
    def _():
        epilogue(extra, o_ref, acc_ref)


def _normalised_t(acc_ref, c):
    dv = acc_ref.shape[1] - ONES_ROWS
    return acc_ref[c, 0:dv, :] / acc_ref[c, dv:dv + 1, :]


def _da_epilogue(extra, o_ref, acc_ref, *, lambda_init):
    lq1_ref, lk1_ref, lq2_ref, lk2_ref, g_ref = extra
    lam = (jnp.exp(jnp.sum(lq1_ref[...] * lk1_ref[...], keepdims=True))
           - jnp.exp(jnp.sum(lq2_ref[...] * lk2_ref[...], keepdims=True)) + lambda_init)
    o = (_normalised_t(acc_ref, 0) - lam * _normalised_t(acc_ref, 1)).T
    inv = lax.rsqrt(jnp.mean(o * o, axis=-1, keepdims=True) + RMS_EPS)
    o_ref[...] = ((o * inv * g_ref[...]) * (1.0 - lambda_init)).astype(o_ref.dtype)


def _mla_epilogue(extra, o_ref, acc_ref):
    o_ref[...] = _normalised_t(acc_ref, 0).T.astype(o_ref.dtype)


def _flash_attention(q, k, v_t, prefix, extra, extra_specs, *, dk, n_comp, epilogue, tq, tk, chunk, name):
    nq, nk = q.shape[0], k.shape[0]
    dv = v_t.shape[0] // HEADS
    tq, tk = _row_tile(nq, tq), _row_tile(nk, tk)
    chunk = _row_tile(tk, chunk)
    has_prefix = prefix is not None
    qspec = pl.BlockSpec((tq, dk), lambda h, i, j: (i, h))
    in_specs, args = [qspec], [q]
    if has_prefix:
        nc = prefix[0].shape[0]
        in_specs += [pl.BlockSpec((nc, dk), lambda h, i, j: (0, h)),
                     pl.BlockSpec((dv, nc), lambda h, i, j: (h, 0))]
        args += list(prefix)
    in_specs += [pl.BlockSpec((tk, dk), lambda h, i, j: (j, h)),
                 pl.BlockSpec((dv, tk), lambda h, i, j: (h, j))] + list(extra_specs)
    args += [k, v_t] + list(extra)
    return pl.pallas_call(
        functools.partial(_flash_kernel, n_comp=n_comp, has_prefix=has_prefix, chunk=chunk, epilogue=epilogue),
        grid=(HEADS, nq // tq, nk // tk),
        in_specs=in_specs,
        out_specs=pl.BlockSpec((tq, dv), lambda h, i, j: (i, h)),
        out_shape=jax.ShapeDtypeStruct((nq, HEADS * dv), BF16),
        scratch_shapes=[
            pltpu.VMEM((n_comp, dk, tq), BF16),
            pltpu.VMEM((n_comp, 1, tq), F32),
            pltpu.VMEM((n_comp, dv + ONES_ROWS, tq), F32),
        ],
        compiler_params=_cparams("arbitrary", "arbitrary", "arbitrary"),
        name=name,
    )(*args)


def _diff_attention(q, k, v_t, prefix, lams, g, lambda_init, tq, tk, chunk):
    small = lambda w: pl.BlockSpec((1, w), lambda h, i, j: (0, 0))
    return _flash_attention(
        q, k, v_t, prefix, list(lams) + [g], [small(DA_HEAD_DIM)] * 4 + [small(LANES)],
        dk=2 * DA_HEAD_DIM, n_comp=2, epilogue=functools.partial(_da_epilogue, lambda_init=lambda_init),
        tq=tq, tk=tk, chunk=chunk, name="diff_attention")


def _mla_attention(q, k, v_t, prefix, tq, tk, chunk):
    return _flash_attention(q, k, v_t, prefix, [], [], dk=MLA_HEAD_PAD, n_comp=1, epilogue=_mla_epilogue,
                            tq=tq, tk=tk, chunk=chunk, name="mla_attention")


def _rms(x, g):
    inv = lax.rsqrt(jnp.mean(x * x, axis=-1, keepdims=True) + RMS_EPS)
    return x * inv * g


def _mla_prep_kernel(u_ref, cos_ref, sin_ref, qg_ref, kvg_ref, wq_ref, wk_ref, wv_ref,
                     q_ref, k_ref, v_ref, *, scale):
    cos = cos_ref[...]
    sin = sin_ref[...]
    cq = _rms(u_ref[:, 0:MLA_Q_RANK], qg_ref[...]).astype(BF16)
    ckv = _rms(u_ref[:, MLA_Q_RANK:MLA_Q_RANK + MLA_KV_RANK], kvg_ref[...]).astype(BF16)
    k_rope = _rope_slab(u_ref[:, MLA_Q_RANK + MLA_KV_RANK:MLA_SEG], cos, sin).astype(BF16)
    k_nope = jnp.dot(ckv, wk_ref[...], preferred_element_type=F32)
    for h in range(HEADS):
        o = h * MLA_HEAD_PAD
        q_h = jnp.dot(cq, wq_ref[:, o:o + MLA_HEAD_PAD], preferred_element_type=F32)
        q_ref[:, o:o + LANES] = (q_h[:, :LANES] * scale).astype(BF16)
        q_ref[:, o + LANES:o + 2 * LANES] = (_rope_slab(q_h[:, LANES:], cos, sin) * scale).astype(BF16)
        k_ref[:, o:o + LANES] = k_nope[:, h * LANES:(h + 1) * LANES].astype(BF16)
        k_ref[:, o + LANES:o + 2 * LANES] = k_rope
    v_ref[...] = jnp.dot(ckv, wv_ref[...], preferred_element_type=F32).T.astype(BF16)


def _mla_prep(u_mla, cos, sin, qn_g, kvn_g, wq_p, wk_p, wv_p, tm):
    n = u_mla.shape[0]
    tm = _row_tile(n, tm)
    row = lambda i: (i, 0)
    fixed = lambda i: (0, 0)
    outs = [
        jax.ShapeDtypeStruct((n, HEADS * MLA_HEAD_PAD), BF16),
        jax.ShapeDtypeStruct((n, HEADS * MLA_HEAD_PAD), BF16),
        jax.ShapeDtypeStruct((GROUP_WIDTH, n), BF16),
    ]
    out_specs = [pl.BlockSpec((tm, HEADS * MLA_HEAD_PAD), row)] * 2 + [pl.BlockSpec((GROUP_WIDTH, tm), lambda i: (0, i))]
    return pl.pallas_call(
        functools.partial(_mla_prep_kernel, scale=(MLA_NOPE + MLA_ROPE) ** -0.5 * LOG2E),
        grid=(n // tm,),
        in_specs=[
            pl.BlockSpec((tm, MLA_SEG), row),
            pl.BlockSpec((tm, LANES), row),
            pl.BlockSpec((tm, LANES), row),
            pl.BlockSpec((1, MLA_Q_RANK), fixed),
            pl.BlockSpec((1, MLA_KV_RANK), fixed),
            pl.BlockSpec(wq_p.shape, fixed),
            pl.BlockSpec(wk_p.shape, fixed),
            pl.BlockSpec(wv_p.shape, fixed),
        ],
        out_specs=out_specs,
        out_shape=outs,
        compiler_params=_cparams("arbitrary"),
        name="mla_prep",
    )(u_mla, cos, sin, qn_g, kvn_g, wq_p, wk_p, wv_p)


def _shift_rows(x, d, halo):
    tt = x.shape[0]
    row8 = lax.broadcasted_iota(jnp.int32, (SUBLANES, x.shape[1]), 0)
    if d > 0:
        r = pltpu.roll(x, d, 0)
        f = pltpu.roll(halo, d, 0)
        first = jnp.where(row8 < d, f, r[:SUBLANES])
        return jnp.concatenate([first, r[SUBLANES:]], axis=0) if tt > SUBLANES else first
    r = pltpu.roll(x, tt + d, 0)
    f = pltpu.roll(halo, SUBLANES + d, 0)
    last = jnp.where(row8 >= SUBLANES + d, f, r[tt - SUBLANES:])
    return jnp.concatenate([r[:tt - SUBLANES], last], axis=0) if tt > SUBLANES else last


def _lru_kernel(*refs, reverse, final):
    if final:
        (x_ref, prev_ref, next_ref, cw_ref, cb_ref, wr_ref, br_ref, wi_ref, bi_ref, lam_ref, h0_ref,
         gate_ref, hf_ref, y_ref, hlast_ref, carry_ref) = refs
    else:
        (x_ref, prev_ref, next_ref, cw_ref, cb_ref, wr_ref, br_ref, wi_ref, bi_ref, lam_ref, h0_ref,
         y_ref, hlast_ref, carry_ref) = refs
    i = pl.program_id(0)
    nt = pl.num_programs(0)
    t = (nt - 1 - i) if reverse else i
    x = x_ref[...]
    tt = x.shape[0]

    @pl.when(i == 0)
    def _():
        carry_ref[...] = h0_ref[...]

    prev = prev_ref[...] * (t > 0).astype(F32)
    nxt = next_ref[...] * (t < nt - 1).astype(F32)
    cw = cw_ref[...]
    xc = (cw[0:1] * _shift_rows(x, 2, prev) + cw[1:2] * _shift_rows(x, 1, prev)
          + cw[2:3] * x + cw[3:4] * _shift_rows(x, -1, nxt)) + cb_ref[...]

    xb = xc.astype(BF16)

    def gate(w_ref, b_ref):
        z = jnp.concatenate(
            [jnp.dot(xb[:, b * LRU_BLOCK:(b + 1) * LRU_BLOCK], w_ref[b], preferred_element_type=F32)
             for b in range(LRU_BLOCKS)], axis=1)
        return jax.nn.sigmoid(z + b_ref[...])

    r = gate(wr_ref, br_ref)
    ig = gate(wi_ref, bi_ref)
    log_a = -LRU_C * r * jax.nn.softplus(-lam_ref[...])
    a = jnp.exp(log_a)
    u = jnp.sqrt(1.0 - jnp.exp(2.0 * log_a)) * (ig * xc)

    row = lax.broadcasted_iota(jnp.int32, a.shape, 0)
    big_a, big_b = a, u
    d = 1
    while d < tt:
        if reverse:
            valid = row < tt - d
            a_s = pltpu.roll(big_a, tt - d, 0)
            b_s = pltpu.roll(big_b, tt - d, 0)
        else:
            valid = row >= d
            a_s = pltpu.roll(big_a, d, 0)
            b_s = pltpu.roll(big_b, d, 0)
        big_b = jnp.where(valid, big_a * b_s + big_b, big_b)
        big_a = jnp.where(valid, big_a * a_s, big_a)
        d *= 2
    h = big_a * carry_ref[...] + big_b
    edge = h[0:1] if reverse else h[tt - 1:tt]
    carry_ref[...] = edge
    hlast_ref[...] = edge
    if final:
        y_ref[...] = ((hf_ref[...] + h) * jax.nn.gelu(gate_ref[...])).astype(y_ref.dtype)
    else:
        y_ref[...] = h


def _lru_pass(u_lru, conv_w, conv_b, wr, br, wi, bi, lam, h0, h_fwd, reverse, tt):
    n = u_lru.shape[0]
    tt = _row_tile(n, tt)
    nt = n // tt
    per8 = tt // SUBLANES
    final = h_fwd is not None
    pos = (lambda i: nt - 1 - i) if reverse else (lambda i: i)
    tile = lambda i: (pos(i), 0)
    fixed = lambda i: (0, 0)
    fixed3 = lambda i: (0, 0, 0)
    in_specs = [
        pl.BlockSpec((tt, GROUP_WIDTH), tile),
        pl.BlockSpec((SUBLANES, GROUP_WIDTH), lambda i: (jnp.maximum(pos(i) * per8 - 1, 0), 0)),
        pl.BlockSpec((SUBLANES, GROUP_WIDTH), lambda i: (jnp.minimum((pos(i) + 1) * per8, n // SUBLANES - 1), 0)),
        pl.BlockSpec((CONV_W, GROUP_WIDTH), fixed),
        pl.BlockSpec((1, GROUP_WIDTH), fixed),
        pl.BlockSpec((LRU_BLOCKS, LRU_BLOCK, LRU_BLOCK), fixed3),
        pl.BlockSpec((1, GROUP_WIDTH), fixed),
        pl.BlockSpec((LRU_BLOCKS, LRU_BLOCK, LRU_BLOCK), fixed3),
        pl.BlockSpec((1, GROUP_WIDTH), fixed),
        pl.BlockSpec((1, GROUP_WIDTH), fixed),
        pl.BlockSpec((1, GROUP_WIDTH), fixed),
    ]
    args = [u_lru, u_lru, u_lru, conv_w, conv_b, wr, br, wi, bi, lam, h0]
    if final:
        in_specs += [pl.BlockSpec((tt, GROUP_WIDTH), lambda i: (pos(i), 1)),
                     pl.BlockSpec((tt, GROUP_WIDTH), tile)]
        args += [u_lru, h_fwd]
    return pl.pallas_call(
        functools.partial(_lru_kernel, reverse=reverse, final=final),
        grid=(nt,),
        in_specs=in_specs,
        out_specs=[pl.BlockSpec((tt, GROUP_WIDTH), tile), pl.BlockSpec((1, GROUP_WIDTH), fixed)],
        out_shape=[jax.ShapeDtypeStruct((n, GROUP_WIDTH), BF16 if final else F32),
                   jax.ShapeDtypeStruct((1, GROUP_WIDTH), F32)],
        scratch_shapes=[pltpu.VMEM((1, GROUP_WIDTH), F32)],
        compiler_params=_cparams("arbitrary"),
        name="rglru_bwd" if reverse else "rglru_fwd",
    )(*args)


def _dft_tables(n):
    k = np.arange(n, dtype=np.int64)
    ang = 2.0 * np.pi * ((k[:, None] * k[None, :]) % n).astype(np.float64) / n
    return np.cos(ang), np.sin(ang)


def _bf16_table(table):
    return jnp.asarray(table, F32).astype(BF16)


def _fft_small_kernel(g_ref, cs_ref, cn_ref, sn_ref, o_ref, *, norm):
    g = g_ref[...]
    a_parts, b_parts = [], []
    for grp in range(FFT_GROUPS):
        gg = g[:, grp * FFT_GROUP:(grp + 1) * FFT_GROUP]
        ab = jnp.dot(gg, cs_ref[...], preferred_element_type=F32)
        a_parts.append(ab[:, :FFT_GROUP])
        b_parts.append(ab[:, FFT_GROUP:])
    a = jnp.concatenate(a_parts, axis=1).astype(BF16)
    b = jnp.concatenate(b_parts, axis=1).astype(BF16)
    y = (jnp.dot(cn_ref[...], a, preferred_element_type=F32)
         - jnp.dot(sn_ref[...], b, preferred_element_type=F32))
    o_ref[...] = (y * norm).astype(o_ref.dtype)


def _fourier_small(g):
    n = g.shape[0]
    cc, sc = _dft_tables(FFT_GROUP)
    cn, sn = _dft_tables(n)
    cs = _bf16_table(np.concatenate([cc, sc], axis=1))
    return pl.pallas_call(
        functools.partial(_fft_small_kernel, norm=float((n * FFT_GROUP) ** -0.5)),
        out_shape=jax.ShapeDtypeStruct((n, GROUP_WIDTH), BF16),
        compiler_params=pltpu.CompilerParams(vmem_limit_bytes=VMEM_LIMIT),
        name="fourier_ctx",
    )(g, cs, _bf16_table(cn), _bf16_table(sn))


def _fft_stage1_kernel(f_ref, x_ref, z_ref):
    res = jnp.dot(f_ref[...], x_ref[...], preferred_element_type=F32)
    tj = z_ref.shape[2]
    res = res.reshape(2, FFT_N1, tj * GROUP_WIDTH)
    for jj in range(tj):
        z_ref[:, :, jj, :] = res[:, :, jj * GROUP_WIDTH:(jj + 1) * GROUP_WIDTH]


def _fft_stage2_kernel(z_ref, twr_ref, twi_ref, f2_ref, cs_ref, o_ref, *, norm):
    tk1 = z_ref.shape[1]
    for i in range(tk1):
        zr = z_ref[0, i]
        zi = z_ref[1, i]
        twr = jnp.concatenate([twr_ref[i]] * (GROUP_WIDTH // LANES), axis=1)
        twi = jnp.concatenate([twi_ref[i]] * (GROUP_WIDTH // LANES), axis=1)
        zz = jnp.concatenate([zr * twr - zi * twi, zr * twi + zi * twr], axis=0).astype(BF16)
        p = jnp.dot(f2_ref[...], zz, preferred_element_type=F32)
        n2 = p.shape[0] // 2
        pr = p[:n2].astype(BF16)
        pim = p[n2:].astype(BF16)
        outs = []
        for grp in range(FFT_GROUPS):
            sl = slice(grp * FFT_GROUP, (grp + 1) * FFT_GROUP)
            lhs = jnp.concatenate([pr[:, sl], pim[:, sl]], axis=1)
            outs.append(jnp.dot(lhs, cs_ref[...], preferred_element_type=F32))
        o_ref[:, i, :] = (jnp.concatenate(outs, axis=1) * norm).astype(o_ref.dtype)


def _fourier_long(g):
    n = g.shape[0]
    n1 = FFT_N1
    assert n % (n1 * SUBLANES) == 0, n
    n2 = n // n1
    c1, s1 = _dft_tables(n1)
    f1 = _bf16_table(np.concatenate([c1, -s1], axis=0))
    tj = min(n2, SUBLANES)
    z = pl.pallas_call(
        _fft_stage1_kernel,
        grid=(n2 // tj,),
        in_specs=[pl.BlockSpec((2 * n1, n1), lambda j: (0, 0)),
                  pl.BlockSpec((n1, tj * GROUP_WIDTH), lambda j: (0, j))],
        out_specs=pl.BlockSpec((2, n1, tj, GROUP_WIDTH), lambda j: (0, 0, j, 0)),
        out_shape=jax.ShapeDtypeStruct((2, n1, n2, GROUP_WIDTH), F32),
        compiler_params=_cparams("arbitrary"),
        name="fourier_stage1",
    )(f1, g.reshape(n1, n2 * GROUP_WIDTH))

    k1 = jnp.arange(n1, dtype=F32)[:, None]
    j2 = jnp.arange(n2, dtype=F32)[None, :]
    ang = (2.0 * np.pi / n) * (k1 * j2)
    twr = jnp.broadcast_to(jnp.cos(ang)[:, :, None], (n1, n2, LANES))
    twi = jnp.broadcast_to(-jnp.sin(ang)[:, :, None], (n1, n2, LANES))
    c2, s2 = _dft_tables(n2)
    f2 = _bf16_table(np.block([[c2, s2], [-s2, c2]]))
    cc, sc = _dft_tables(FFT_GROUP)
    cs = _bf16_table(np.concatenate([cc, sc], axis=0))
    tk1 = SUBLANES
    out = pl.pallas_call(
        functools.partial(_fft_stage2_kernel, norm=float((n * FFT_GROUP) ** -0.5)),
        grid=(n1 // tk1,),
        in_specs=[pl.BlockSpec((2, tk1, n2, GROUP_WIDTH), lambda i: (0, i, 0, 0)),
                  pl.BlockSpec((tk1, n2, LANES), lambda i: (i, 0, 0)),
                  pl.BlockSpec((tk1, n2, LANES), lambda i: (i, 0, 0)),
                  pl.BlockSpec((2 * n2, 2 * n2), lambda i: (0, 0)),
                  pl.BlockSpec((2 * FFT_GROUP, FFT_GROUP), lambda i: (0, 0))],
        out_specs=pl.BlockSpec((n2, tk1, GROUP_WIDTH), lambda i: (0, i, 0)),
        out_shape=jax.ShapeDtypeStruct((n2, n1, GROUP_WIDTH), BF16),
        compiler_params=_cparams("arbitrary"),
        name="fourier_stage2",
    )(z, twr, twi, f2, cs)
    return out.reshape(n, GROUP_WIDTH)


def _layernorm(z, g, b):
    mu = jnp.mean(z, axis=-1, keepdims=True)
    zc = z - mu
    var = jnp.mean(zc * zc, axis=-1, keepdims=True)
    return zc * lax.rsqrt(var + LN_EPS) * g + b


def _outproj_kernel(da_ref, lru_ref, mla_ref, fft_ref, x_ref, gate_ref, w_ref, g_ref, b_ref, o_ref, *, alpha):
    y = jnp.dot(da_ref[...], w_ref[0:GROUP_WIDTH], preferred_element_type=F32)
    y += jnp.dot(lru_ref[...], w_ref[GROUP_WIDTH:2 * GROUP_WIDTH], preferred_element_type=F32)
    y += jnp.dot(mla_ref[...], w_ref[2 * GROUP_WIDTH:3 * GROUP_WIDTH], preferred_element_type=F32)
    y += jnp.dot(fft_ref[...], w_ref[3 * GROUP_WIDTH:4 * GROUP_WIDTH], preferred_element_type=F32)
    z = alpha * x_ref[...] + gate_ref[...] * y
    o_ref[...] = _layernorm(z, g_ref[...], b_ref[...])


def _outproj_ln(parts, x, gate, w_out, layer, ln_g, ln_b, alpha, tm):
    n, d = x.shape
    tm = _row_tile(n, tm)
    row = lambda i: (i, 0)
    fixed = lambda i: (0, 0)
    return pl.pallas_call(
        functools.partial(_outproj_kernel, alpha=alpha),
        grid=(n // tm,),
        in_specs=[pl.BlockSpec((tm, GROUP_WIDTH), row)] * 4 + [
            pl.BlockSpec((tm, d), row),
            pl.BlockSpec((1, d), fixed),
            pl.BlockSpec((None,) + w_out.shape[1:], lambda i: (layer, 0, 0)),
            pl.BlockSpec((1, d), fixed),
            pl.BlockSpec((1, d), fixed),
        ],
        out_specs=pl.BlockSpec((tm, d), row),
        out_shape=jax.ShapeDtypeStruct((n, d), F32),
        compiler_params=_cparams("arbitrary"),
        name="out_proj_ln",
    )(*parts, x, gate, w_out, ln_g, ln_b)


def _ffn_kernel(x_ref, sh_ref, sc_ref, gate_ref, wg_ref, wu_ref, wd_ref, g_ref, b_ref, o_ref,
                h_ref, *, alpha):
    f = pl.program_id(1)

    @pl.when(f == 0)
    def _():
        h_ref[...] = (x_ref[...] * (1.0 + sc_ref[...]) + sh_ref[...]).astype(BF16)
        o_ref[...] = jnp.zeros(o_ref.shape, F32)

    h = h_ref[...]
    sub = FFN_SUB
    contrib = None
    for s in range(wg_ref.shape[1] // sub):
        cols = slice(s * sub, (s + 1) * sub)
        a = jnp.dot(h, wg_ref[:, cols], preferred_element_type=F32)
        u = jnp.dot(h, wu_ref[:, cols], preferred_element_type=F32)
        act = (a * jax.nn.sigmoid(a) * u).astype(BF16)
        part = jnp.dot(act, wd_ref[cols, :], preferred_element_type=F32)
        contrib = part if contrib is None else contrib + part
    o_ref[...] += contrib

    @pl.when(f == pl.num_programs(1) - 1)
    def _():
        z = alpha * x_ref[...] + gate_ref[...] * o_ref[...]
        o_ref[...] = _layernorm(z, g_ref[...], b_ref[...])


def _ffn_ln(x, shift, scale, gate, wg, wu, wd, layer, ln_g, ln_b, alpha, tm, tf):
    n, d = x.shape
    d_ff = wg.shape[2]
    tm = _row_tile(n, tm)
    assert d_ff % tf == 0
    row = lambda i, f: (i, 0)
    fixed = lambda i, f: (0, 0)
    return pl.pallas_call(
        functools.partial(_ffn_kernel, alpha=alpha),
        grid=(n // tm, d_ff // tf),
        in_specs=[
            pl.BlockSpec((tm, d), row),
            pl.BlockSpec((1, d), fixed),
            pl.BlockSpec((1, d), fixed),
            pl.BlockSpec((1, d), fixed),
            pl.BlockSpec((None, d, tf), lambda i, f: (layer, 0, f)),
            pl.BlockSpec((None, d, tf), lambda i, f: (layer, 0, f)),
            pl.BlockSpec((None, tf, d), lambda i, f: (layer, f, 0)),
            pl.BlockSpec((1, d), fixed),
            pl.BlockSpec((1, d), fixed),
        ],
        out_specs=pl.BlockSpec((tm, d), row),
        out_shape=jax.ShapeDtypeStruct((n, d), F32),
        scratch_shapes=[pltpu.VMEM((tm, d), BF16)],
        compiler_params=_cparams("arbitrary", "arbitrary"),
        name="ffn_ln",
    )(x, shift, scale, gate, wg, wu, wd, ln_g, ln_b)


def _split_in_weight(w_in):
    src_fft = 3 * GROUP_WIDTH + 2 * GROUP_WIDTH + MLA_Q_RANK + MLA_KV_RANK + MLA_ROPE
    main = jnp.pad(w_in[..., :src_fft].astype(BF16), ((0, 0), (0, 0), (0, SEG_FFT - src_fft)))
    return main, w_in[..., src_fft:].astype(BF16)


def _pad_mla_weights(w_uq, w_ukv):
    qr = w_uq.shape[0]
    wq = w_uq.reshape(qr, HEADS, MLA_NOPE + MLA_ROPE)
    wq = jnp.concatenate([wq, jnp.zeros((qr, HEADS, MLA_HEAD_PAD - MLA_NOPE - MLA_ROPE), w_uq.dtype)], axis=2)
    wkv = w_ukv.reshape(w_ukv.shape[0], HEADS, MLA_NOPE + MLA_V)
    wk = wkv[:, :, :MLA_NOPE].reshape(w_ukv.shape[0], HEADS * MLA_NOPE)
    wv = wkv[:, :, MLA_NOPE:].reshape(w_ukv.shape[0], HEADS * MLA_V)
    return (wq.reshape(qr, HEADS * MLA_HEAD_PAD).astype(BF16), wk.astype(BF16), wv.astype(BF16))


def _rope_tables(n):
    rows = n // GRID_W
    axis_dim = ROT_DIM // 2
    inv = ROPE_BASE ** (-jnp.arange(0, axis_dim, 2, dtype=F32) / axis_dim)
    ang_row = jnp.arange(rows, dtype=F32)[:, None] * inv
    ang_col = jnp.arange(GRID_W, dtype=F32)[:, None] * inv

    def slab(fn, sign_first):
        r, c = fn(ang_row), fn(ang_col)
        zr, zc = jnp.zeros_like(r), jnp.zeros_like(c)
        reps = LANES // ROT_DIM
        by_row = jnp.concatenate([sign_first * r, zr, r, zr] * reps, axis=1)
        by_col = jnp.concatenate([zc, sign_first * c, zc, c] * reps, axis=1)
        return (by_row[:, None, :] + by_col[None, :, :]).reshape(n, LANES)

    return slab(jnp.cos, 1.0), slab(jnp.sin, -1.0)


def kernel(x, c, ctx, c_ctx, w_ada, b_ada, w_in, w_out, ln1_g, ln1_b, ln2_g, ln2_b,
           da_lq1, da_lk1, da_lq2, da_lk2, da_subln_g,
           lru_conv_w, lru_conv_b, lru_wr, lru_br, lru_wi, lru_bi, lru_lam,
           mla_qn_g, mla_wuq, mla_kvn_g, mla_wukv,
           ffn_wg, ffn_wu, ffn_wd):
    assert x.shape[0] == 1 and c.shape[0] == 1 and ctx.shape[0] == 1
    depth = w_ada.shape[0]
    n, d = x.shape[1], x.shape[2]
    nc = ctx.shape[1]
    alpha = (2 * depth) ** 0.25
    x_lat, x_ctx = x[0], ctx[0]

    cos_l, sin_l = _rope_tables(n)
    cos_c, sin_c = jnp.ones((nc, LANES), F32), jnp.zeros((nc, LANES), F32)

    c8 = jnp.concatenate([c, c_ctx[None], jnp.zeros((SUBLANES - 2, d), F32)], axis=0)
    mods = _ada_mods(c8, w_ada, b_ada)

    w_main, w_fft = _split_in_weight(w_in)
    w_out_b = w_out.astype(BF16)
    wg_b, wu_b, wd_b = ffn_wg.astype(BF16), ffn_wu.astype(BF16), ffn_wd.astype(BF16)

    row2 = lambda v: v.reshape(1, -1)
    for l in range(depth):
        need_ctx = l < depth - 1
        lambda_init = 0.8 - 0.6 * math.exp(-0.3 * l)
        m_lat = [mods[l, 0:1, i * d:(i + 1) * d] for i in range(6)]
        m_ctx = [mods[l, 1:2, i * d:(i + 1) * d] for i in range(6)]

        wq_p, wk_p, wv_p = _pad_mla_weights(mla_wuq[l], mla_wukv[l])

        ql, kl, vl, lru_l, mla_l, fft_l = _inproj(x_lat, m_lat[0], m_lat[1], cos_l, sin_l, w_main, w_fft, l, 512)
        qc, kc, vc, lru_c, mla_c, fft_c = _inproj(x_ctx, m_ctx[0], m_ctx[1], cos_c, sin_c, w_main, w_fft, l, 256)

        lams = [row2(da_lq1[l]), row2(da_lk1[l]), row2(da_lq2[l]), row2(da_lk2[l])]
        g_da = row2(da_subln_g[l])
        da_l = _diff_attention(ql, kl, vl, (kc, vc), lams, g_da, lambda_init, ATTN_TQ, ATTN_TK, ATTN_CHUNK)

        lru_args = lambda dr: (lru_conv_w[l], row2(lru_conv_b[l]), lru_wr[l, dr].astype(BF16), row2(lru_br[l, dr]),
                               lru_wi[l, dr].astype(BF16), row2(lru_bi[l, dr]), row2(lru_lam[l, dr]))
        h0 = jnp.zeros((1, GROUP_WIDTH), F32)
        hc_f, s_f = _lru_pass(lru_c, *lru_args(0), h0, None, False, 256)
        lru_yc, s_b = _lru_pass(lru_c, *lru_args(1), h0, hc_f, True, 256)
        hl_f, _ = _lru_pass(lru_l, *lru_args(0), s_f, None, False, 256)
        lru_yl, _ = _lru_pass(lru_l, *lru_args(1), s_b, hl_f, True, 256)

        mq_l, mk_l, mv_l = _mla_prep(mla_l, cos_l, sin_l, row2(mla_qn_g[l]), row2(mla_kvn_g[l]), wq_p, wk_p, wv_p, 512)
        mq_c, mk_c, mv_c = _mla_prep(mla_c, cos_c, sin_c, row2(mla_qn_g[l]), row2(mla_kvn_g[l]), wq_p, wk_p, wv_p, 256)
        mla_yl = _mla_attention(mq_l, mk_l, mv_l, (mk_c, mv_c), 2 * ATTN_TQ, ATTN_TK, ATTN_CHUNK)

        fft_yl = _fourier_long(fft_l)

        x_lat = _outproj_ln([da_l, lru_yl, mla_yl, fft_yl], x_lat, m_lat[2], w_out_b, l,
                            row2(ln1_g[l]), row2(ln1_b[l]), alpha, 512)
        x_lat = _ffn_ln(x_lat, m_lat[3], m_lat[4], m_lat[5], wg_b, wu_b, wd_b, l,
                        row2(ln2_g[l]), row2(ln2_b[l]), alpha, 1024, 256)

        if need_ctx:
            da_c = _diff_attention(qc, kc, vc, None, lams, g_da, lambda_init, 256, 256, 256)
            mla_yc = _mla_attention(mq_c, mk_c, mv_c, None, 256, 256, 256)
            fft_yc = _fourier_small(fft_c)
            x_ctx = _outproj_ln([da_c, lru_yc, mla_yc, fft_yc], x_ctx, m_ctx[2], w_out_b, l,
                                row2(ln1_g[l]), row2(ln1_b[l]), alpha, 256)
            x_ctx = _ffn_ln(x_ctx, m_ctx[3], m_ctx[4], m_ctx[5], wg_b, wu_b, wd_b, l,
                            row2(ln2_g[l]), row2(ln2_b[l]), alpha, 256, 512)
    return x_lat[None]
```
